```python
import jax, jax.numpy as jnp
from jax import lax
import numpy as np

D_MODEL = 1024
BATCH = 4
SEQ = 8192
DEPTH = 2
DEC_BATCH = 32
DEC_SEQ = 8
PAST_LEN = 16384
PAGE_SIZE = 128

N_HEADS = 16
HEAD_DIM = D_MODEL // N_HEADS
D_FF = D_MODEL * 7 // 2
N_EXPERTS = 8
TOP_K = 2
D_FF_EXPERT = D_MODEL * 7 // 2
Q_BLOCK = 128
RMS_EPS = 1e-6
FGATE_BIAS_LO = 1.0
FGATE_BIAS_HI = 6.0

kernel_name = "stickbreak_fox_hybrid_decode_step"


def rmsnorm(x, g):
    xf = x.astype(jnp.float32)
    y = xf * lax.rsqrt(jnp.mean(xf * xf, axis=-1, keepdims=True) + RMS_EPS)
    return (y * g.astype(jnp.float32)).astype(x.dtype)


def swiglu(x, w_gate, w_up, w_down):
    return (jax.nn.silu(x @ w_gate) * (x @ w_up)) @ w_down


def moe_swiglu(x, router, w_gate, w_up, w_down):
    b, s, d = x.shape
    xt = x.reshape(b * s, d)
    logits = (xt @ router).astype(jnp.float32)
    top_v, top_i = lax.top_k(logits, TOP_K)
    top_w = jax.nn.softmax(top_v, axis=-1)
    gate = jnp.sum(jax.nn.one_hot(top_i, N_EXPERTS, dtype=jnp.float32) * top_w[..., None], axis=1)
    y = jnp.zeros_like(xt)
    for e in range(N_EXPERTS):
        y = y + gate[:, e:e + 1].astype(x.dtype) * swiglu(xt, w_gate[e], w_up[e], w_down[e])
    return y.reshape(b, s, d)


def split_heads(a):
    return a.reshape(a.shape[:2] + (N_HEADS, HEAD_DIM))


def gather_pages(cache, page_table):
    pages = cache[page_table]
    b, n, p = pages.shape[:3]
    return pages.reshape((b, n * p) + pages.shape[3:])


def block_size(t):
    return Q_BLOCK if t % Q_BLOCK == 0 else t


def to_blocks(a, blk):
    b, t = a.shape[:2]
    return jnp.moveaxis(a.reshape((b, t // blk, blk) + a.shape[2:]), 1, 0)


def from_blocks(a):
    a = jnp.moveaxis(a, 0, 1)
    return a.reshape((a.shape[0], a.shape[1] * a.shape[2]) + a.shape[3:])


def stick_breaking_attention(q, k, v):
    tq, tk = q.shape[1], k.shape[1]
    scale = HEAD_DIM ** -0.5
    k_pos = jnp.arange(tk, dtype=jnp.int32)
    q_pos = jnp.arange(tk - tq, tk, dtype=jnp.int32)
    blk = block_size(tq)

    def one_block(args):
        qb, pb = args
        z = jnp.einsum("bqhd,bkhd->bhqk", qb, k, preferred_element_type=jnp.float32) * scale
        mask = k_pos[None, :] < pb[:, None]
        log_keep = jnp.where(mask, -jax.nn.softplus(z), 0.0)
        log_later = lax.cumsum(log_keep, axis=3, reverse=True) - log_keep
        a = jnp.where(mask, jnp.exp(jax.nn.log_sigmoid(z) + log_later), 0.0)
        return jnp.einsum("bhqk,bkhd->bqhd", a.astype(v.dtype), v)

    out = lax.map(one_block, (to_blocks(q, blk), q_pos.reshape(-1, blk)))
    return from_blocks(out)


def forgetting_attention(q, k, v, cum_logf):
    tq, tk = q.shape[1], k.shape[1]
    scale = HEAD_DIM ** -0.5
    k_pos = jnp.arange(tk, dtype=jnp.int32)
    q_pos = jnp.arange(tk - tq, tk, dtype=jnp.int32)
    c_k = jnp.transpose(cum_logf, (0, 2, 1))
    c_q = cum_logf[:, tk - tq:]
    blk = block_size(tq)

    def one_block(args):
        qb, pb, cqb = args
        z = jnp.einsum("bqhd,bkhd->bhqk", qb, k, preferred_element_type=jnp.float32) * scale
        z = z + jnp.transpose(cqb, (0, 2, 1))[..., None] - c_k[:, :, None, :]
        mask = k_pos[None, :] <= pb[:, None]
        p = jax.nn.softmax(jnp.where(mask, z, -jnp.inf), axis=-1)
        return jnp.einsum("bhqk,bkhd->bqhd", p.astype(v.dtype), v)

    out = lax.map(one_block, (to_blocks(q, blk), q_pos.reshape(-1, blk), to_blocks(c_q, blk)))
    return from_blocks(out)


def sb_mixer(h, w_in, w_o, past):
    q, k, v = (split_heads(a) for a in jnp.split(h @ w_in, 3, axis=-1))
    if past is None:
        k_all, v_all = k, v
    else:
        k_all = jnp.concatenate([past[0].astype(k.dtype), k], axis=1)
        v_all = jnp.concatenate([past[1].astype(v.dtype), v], axis=1)
    o = stick_breaking_attention(q, k_all, v_all)
    return o.reshape(h.shape) @ w_o, (k, v)


def fox_mixer(h, w_in, b_f, w_o, past):
    proj = h @ w_in
    q = split_heads(proj[..., :D_MODEL])
    k = split_heads(proj[..., D_MODEL:2 * D_MODEL])
    v = split_heads(proj[..., 2 * D_MODEL:3 * D_MODEL])
    logf = jax.nn.log_sigmoid(proj[..., 3 * D_MODEL:].astype(jnp.float32) + b_f.astype(jnp.float32))
    if past is None:
        k_all, v_all, logf_all = k, v, logf
    else:
        k_all = jnp.concatenate([past[0].astype(k.dtype), k], axis=1)
        v_all = jnp.concatenate([past[1].astype(v.dtype), v], axis=1)
        logf_all = jnp.concatenate([past[2].astype(jnp.float32), logf], axis=1)
    cum_logf = jnp.cumsum(logf_all, axis=1)
    o = forgetting_attention(q, k_all, v_all, cum_logf)
    return o.reshape(h.shape) @ w_o, (k, v, logf)


def setup_inputs(seed: int = 0) -> dict:
    key = jax.random.key(seed)
    ks = jax.random.split(key, 28)
    n_pages = PAST_LEN // PAGE_SIZE
    n_used = DEC_BATCH * n_pages
    n_pool = n_used + n_used // 4
    f32 = jnp.float32
    d = D_MODEL
    s_d = d ** -0.5
    kv_shape = (n_pool, PAGE_SIZE, N_HEADS, HEAD_DIM)
    head_bias = jnp.linspace(FGATE_BIAS_LO, FGATE_BIAS_HI, N_HEADS, dtype=f32)

    def norm_gain(k):
        return 1.0 + 0.02 * jax.random.normal(k, (d,), f32)

    inputs = {
        "x_prompt": jax.random.normal(ks[0], (BATCH, SEQ, d), f32),
        "x_sample": jax.random.normal(ks[1], (DEC_BATCH, DEC_SEQ, d), f32),
        "cache_sb_k": jax.random.normal(ks[2], kv_shape, f32),
        "cache_sb_v": jax.random.normal(ks[3], kv_shape, f32),
        "cache_fox_k": jax.random.normal(ks[4], kv_shape, f32),
        "cache_fox_v": jax.random.normal(ks[5], kv_shape, f32),
        "cache_fox_logf": jax.nn.log_sigmoid(head_bias + 0.5 * jax.random.normal(ks[6], (n_pool, PAGE_SIZE, N_HEADS), f32)),
        "page_table": jax.random.permutation(ks[7], n_pool)[:n_used].reshape(DEC_BATCH, n_pages).astype(jnp.int32),
        "ln_attn0": norm_gain(ks[8]),
        "w_in0": jax.random.normal(ks[9], (d, 3 * d), f32) * s_d,
        "w_o0": jax.random.normal(ks[10], (d, d), f32) * s_d,
        "ln_ffn0": norm_gain(ks[11]),
        "w_gate0": jax.random.normal(ks[12], (d, D_FF), f32) * s_d,
        "w_up0": jax.random.normal(ks[13], (d, D_FF), f32) * s_d,
        "w_down0": jax.random.normal(ks[14], (D_FF, d), f32) * D_FF ** -0.5,
        "ln_attn1": norm_gain(ks[15]),
        "w_in1": jax.random.normal(ks[16], (d, 3 * d + N_HEADS), f32) * s_d,
        "b_f1": head_bias + 0.1 * jax.random.normal(ks[17], (N_HEADS,), f32),
        "w_o1": jax.random.normal(ks[18], (d, d), f32) * s_d,
        "ln_ffn1": norm_gain(ks[19]),
        "router1": jax.random.normal(ks[20], (d, N_EXPERTS), f32) * s_d,
        "w_gate1": jax.random.normal(ks[21], (N_EXPERTS, d, D_FF_EXPERT), f32) * s_d,
        "w_up1": jax.random.normal(ks[22], (N_EXPERTS, d, D_FF_EXPERT), f32) * s_d,
        "w_down1": jax.random.normal(ks[23], (N_EXPERTS, D_FF_EXPERT, d), f32) * D_FF_EXPERT ** -0.5,
        "ln_final": norm_gain(ks[24]),
    }
    return inputs


def reference(x_prompt, x_sample, cache_sb_k, cache_sb_v, cache_fox_k, cache_fox_v, cache_fox_logf, page_table,
              ln_attn0, w_in0, w_o0, ln_ffn0, w_gate0, w_up0, w_down0,
              ln_attn1, w_in1, b_f1, w_o1, ln_ffn1, router1, w_gate1, w_up1, w_down1, ln_final):
    ln_attn = (ln_attn0, ln_attn1)
    ln_ffn = (ln_ffn0, ln_ffn1)
    paged = ((cache_sb_k, cache_sb_v), (cache_fox_k, cache_fox_v, cache_fox_logf))
    xp, xs = x_prompt, x_sample
    new_prompt_state, new_sample_state = [], []
    for i in range(DEPTH):
        hp = rmsnorm(xp, ln_attn[i])
        hs = rmsnorm(xs, ln_attn[i])
        past = tuple(gather_pages(c, page_table) for c in paged[i])
        if i % 2 == 0:
            yp, st_p = sb_mixer(hp, w_in0, w_o0, None)
            ys, st_s = sb_mixer(hs, w_in0, w_o0, past)
        else:
            yp, st_p = fox_mixer(hp, w_in1, b_f1, w_o1, None)
            ys, st_s = fox_mixer(hs, w_in1, b_f1, w_o1, past)
        new_prompt_state.append(st_p)
        new_sample_state.append(st_s)
        xp = xp + yp
        xs = xs + ys
        hp = rmsnorm(xp, ln_ffn[i])
        hs = rmsnorm(xs, ln_ffn[i])
        if i % 2 == 0:
            xp = xp + swiglu(hp, w_gate0, w_up0, w_down0)
            xs = xs + swiglu(hs, w_gate0, w_up0, w_down0)
        else:
            xp = xp + moe_swiglu(hp, router1, w_gate1, w_up1, w_down1)
            xs = xs + moe_swiglu(hs, router1, w_gate1, w_up1, w_down1)
    y_prompt = rmsnorm(xp, ln_final)
    y_sample = rmsnorm(xs, ln_final)
    (sb_k_prompt, sb_v_prompt), (fox_k_prompt, fox_v_prompt, fox_logf_prompt) = new_prompt_state
    (sb_k_sample, sb_v_sample), (fox_k_sample, fox_v_sample, fox_logf_sample) = new_sample_state
    return (y_prompt, y_sample, sb_k_prompt, sb_v_prompt, fox_k_prompt, fox_v_prompt, fox_logf_prompt,
            sb_k_sample, sb_v_sample, fox_k_sample, fox_v_sample, fox_logf_sample)
```

```python
import functools

import jax
import jax.numpy as jnp
from jax import lax
from jax.experimental import pallas as pl
from jax.experimental.pallas import tpu as pltpu

N_HEADS = 16
HEAD_DIM = 64
LANES = 128
HEADS_PER_TILE = LANES // HEAD_DIM
RMS_EPS = 1e-6
NEG_BIG = -1e30
SB_EXIT = 90.0
VMEM_LIMIT = 56 * 1024 * 1024

F32 = jnp.float32
BF16 = jnp.bfloat16

_NT = (((1,), (1,)), ((), ()))


def _dot(a, b):
    return jnp.dot(a, b, preferred_element_type=F32)


def _dot_nt(a, b):
    return lax.dot_general(a, b, _NT, preferred_element_type=F32)


def _split3(x):
    p1 = x.astype(BF16)
    r1 = x - p1.astype(F32)
    p2 = r1.astype(BF16)
    r2 = r1 - p2.astype(F32)
    return p1, p2, r2.astype(BF16)


def _softplus(z):
    return jnp.maximum(z, 0.0) + jnp.log1p(jnp.exp(-jnp.abs(z)))


def _rmsnorm(x, g):
    ms = jnp.mean(x * x, axis=-1, keepdims=True)
    return (x * lax.rsqrt(ms + RMS_EPS)) * g


def _params(*sem):
    return pltpu.CompilerParams(dimension_semantics=sem, vmem_limit_bytes=VMEM_LIMIT)


def _norm_proj_kernel(*refs, d, gate, cumulate, tiles_per_seq):
    x_ref, g_ref, w_ref = refs[:3]
    refs = refs[3:]
    if gate:
        wf_ref, bf_ref = refs[:2]
        refs = refs[2:]
    if cumulate:
        tri_ref = refs[0]
        refs = refs[1:]
    q_ref, k_ref, v_ref, kb_ref, vb_ref = refs[:5]
    refs = refs[5:]

    hb = _rmsnorm(x_ref[...], g_ref[...]).astype(BF16)
    y = _dot(hb, w_ref[...])
    q_ref[...] = (y[:, :d] * (HEAD_DIM ** -0.5)).astype(BF16)
    k = y[:, d:2 * d]
    v = y[:, 2 * d:]
    k_ref[...] = k
    v_ref[...] = v
    kb_ref[...] = k.astype(BF16)
    vb_ref[...] = v.astype(BF16)
    if not gate:
        return
    logf_ref = refs[0]
    lg = _dot(hb, wf_ref[...]) + bf_ref[...]
    logf = jnp.minimum(lg, 0.0) - jnp.log1p(jnp.exp(-jnp.abs(lg)))
    logf_ref[...] = logf[:, :N_HEADS]
    if not cumulate:
        return
    cum_ref, cumt_ref, carry_ref = refs[1:4]

    @pl.when(pl.program_id(0) % tiles_per_seq == 0)
    def _():
        carry_ref[...] = jnp.zeros_like(carry_ref)

    tri = tri_ref[...]
    p1, p2, p3 = _split3(logf)
    cum = (_dot(tri, p1) + _dot(tri, p2)) + _dot(tri, p3) + carry_ref[0:1, :]
    carry_ref[...] = jnp.broadcast_to(cum[-1:, :], carry_ref.shape)
    cum_ref[...] = cum[:, :N_HEADS]
    cumt_ref[...] = cum.T[:N_HEADS, :]


def _norm_proj(x, g, w, wf=None, bf=None, *, tm, seq_len=None):
    t, d = x.shape
    gate = wf is not None
    cumulate = gate and seq_len is not None
    nb = t // seq_len if cumulate else 1
    tiles_per_seq = seq_len // tm if cumulate else 1
    row = lambda i: (i, 0)
    const = lambda i: (0, 0)
    in_specs = [pl.BlockSpec((tm, d), row), pl.BlockSpec((1, d), const),
                pl.BlockSpec((d, 3 * d), const)]
    args = [x, g.reshape(1, d), w]
    if gate:
        in_specs += [pl.BlockSpec((d, LANES), const), pl.BlockSpec((1, LANES), const)]
        args += [wf, bf]
    if cumulate:
        in_specs.append(pl.BlockSpec((tm, tm), const))
        args.append(jnp.tril(jnp.ones((tm, tm), BF16)))
    out_shape = [jax.ShapeDtypeStruct((t, d), BF16), jax.ShapeDtypeStruct((t, d), F32),
                 jax.ShapeDtypeStruct((t, d), F32), jax.ShapeDtypeStruct((t, d), BF16),
                 jax.ShapeDtypeStruct((t, d), BF16)]
    out_specs = [pl.BlockSpec((tm, d), row)] * 5
    scratch = []
    if gate:
        out_shape.append(jax.ShapeDtypeStruct((t, N_HEADS), F32))
        out_specs.append(pl.BlockSpec((tm, N_HEADS), row))
    if cumulate:
        out_shape += [jax.ShapeDtypeStruct((t, N_HEADS), F32),
                      jax.ShapeDtypeStruct((nb, N_HEADS, seq_len), F32)]
        out_specs += [pl.BlockSpec((tm, N_HEADS), row),
                      pl.BlockSpec((None, N_HEADS, tm),
                                   lambda i: (i // tiles_per_seq, 0, i % tiles_per_seq))]
        scratch.append(pltpu.VMEM((8, LANES), F32))
    return pl.pallas_call(
        functools.partial(_norm_proj_kernel, d=d, gate=gate, cumulate=cumulate,
                          tiles_per_seq=tiles_per_seq),
        grid=(t // tm,), in_specs=in_specs, out_specs=out_specs, out_shape=out_shape,
        scratch_shapes=scratch, compiler_params=_params("arbitrary"),
        name="norm_proj_gate" if gate else "norm_proj")(*args)


def _head_masked(q, h):
    lane = lax.broadcasted_iota(jnp.int32, q.shape, 1)
    return jnp.where((lane >= HEAD_DIM) == bool(h), q, jnp.zeros_like(q))


def _sb_tile(qh, ks, vs, u, carry, acc, mask):
    s = _dot_nt(qh, ks)
    lk = -_softplus(s)
    if mask is not None:
        lk = jnp.where(mask, lk, 0.0)
    hi = lk.astype(BF16)
    lo = (lk - hi.astype(F32)).astype(BF16)
    suffix = _dot(hi, u) + _dot(lo, u)
    a = jnp.exp(s + suffix + carry)
    if mask is not None:
        a = jnp.where(mask, a, 0.0)
    acc = acc + _dot(a.astype(BF16), vs)
    return carry + suffix[:, 0:1], acc


def _sb_prompt_kernel(q_ref, k_ref, v_ref, u_ref, o_ref, *, tq):
    i = pl.program_id(2)
    q = q_ref[...]
    u = u_ref[...]
    row = lax.broadcasted_iota(jnp.int32, (tq, tq), 0)
    col = lax.broadcasted_iota(jnp.int32, (tq, tq), 1)
    diag_mask = col < row
    outs = []
    for h in range(HEADS_PER_TILE):
        qh = _head_masked(q, h)

        def tile(j, carry, acc, mask):
            start = pl.multiple_of(j * tq, tq)
            return _sb_tile(qh, k_ref[pl.ds(start, tq), :], v_ref[pl.ds(start, tq), :],
                            u, carry, acc, mask)

        carry, acc = tile(i, jnp.zeros((tq, 1), F32), jnp.zeros((tq, LANES), F32), diag_mask)

        def live(carry):
            return (jnp.max(carry) > -SB_EXIT).astype(jnp.int32)

        def cond(st):
            return jnp.logical_and(st[0] >= 0, st[1] > 0)

        def body(st):
            j, _, carry, acc = st
            carry, acc = tile(j, carry, acc, None)
            return j - 1, live(carry), carry, acc

        acc = lax.while_loop(cond, body, (i - 1, live(carry), carry, acc))[3]
        outs.append(acc)
    lane = lax.broadcasted_iota(jnp.int32, (tq, LANES), 1)
    o_ref[...] = jnp.where(lane < HEAD_DIM, outs[0], outs[1]).astype(BF16)


def _suffix_ones(n):
    return jnp.tril(jnp.ones((n, n), BF16))


def _sb_prompt(q, kb, vb, *, tq):
    b, s, d = q.shape
    tile = pl.BlockSpec((None, tq, LANES), lambda bi, hp, i: (bi, i, hp))
    resident = pl.BlockSpec((None, s, LANES), lambda bi, hp, i: (bi, 0, hp))
    return pl.pallas_call(
        functools.partial(_sb_prompt_kernel, tq=tq),
        grid=(b, d // LANES, s // tq),
        in_specs=[tile, resident, resident, pl.BlockSpec((tq, tq), lambda bi, hp, i: (0, 0))],
        out_specs=tile, out_shape=jax.ShapeDtypeStruct((b, s, d), BF16),
        compiler_params=_params("arbitrary", "arbitrary", "arbitrary"),
        name="sb_prompt")(q, kb, vb, _suffix_ones(tq))


def _softmax_tile(s, vs, m, l, acc):
    m_new = jnp.maximum(m, jnp.max(s, axis=1, keepdims=True))
    alpha = jnp.exp(m - m_new)
    p = jnp.exp(s - m_new)
    l = alpha * l + jnp.sum(p, axis=1, keepdims=True)
    acc = alpha * acc + _dot(p.astype(BF16), vs)
    return m_new, l, acc


def _fox_prompt_kernel(q_ref, k_ref, v_ref, cq_ref, ckt_ref, o_ref, *, tq):
    hp = pl.program_id(1)
    i = pl.program_id(2)
    q = q_ref[...]
    row = lax.broadcasted_iota(jnp.int32, (tq, tq), 0)
    col = lax.broadcasted_iota(jnp.int32, (tq, tq), 1)
    diag_mask = col <= row
    cq_all = cq_ref[...]
    head_lane = lax.broadcasted_iota(jnp.int32, cq_all.shape, 1)
    outs = []
    for h in range(HEADS_PER_TILE):
        qh = _head_masked(q, h)
        hh = hp * HEADS_PER_TILE + h
        cq = jnp.sum(jnp.where(head_lane == hh, cq_all, 0.0), axis=1, keepdims=True)

        def tile(j, m, l, acc, mask):
            start = pl.multiple_of(j * tq, tq)
            s = _dot_nt(qh, k_ref[pl.ds(start, tq), :])
            s = s + (cq - ckt_ref[pl.ds(hh, 1), pl.ds(start, tq)])
            if mask is not None:
                s = jnp.where(mask, s, NEG_BIG)
            return _softmax_tile(s, v_ref[pl.ds(start, tq), :], m, l, acc)

        state = tile(i, jnp.full((tq, 1), NEG_BIG, F32), jnp.zeros((tq, 1), F32),
                     jnp.zeros((tq, LANES), F32), diag_mask)
        _, l, acc = lax.fori_loop(0, i, lambda j, st: tile(j, *st, None), state)
        outs.append(acc / l)
    lane = lax.broadcasted_iota(jnp.int32, (tq, LANES), 1)
    o_ref[...] = jnp.where(lane < HEAD_DIM, outs[0], outs[1]).astype(BF16)


def _fox_prompt(q, kb, vb, cum, cumt, *, tq):
    b, s, d = q.shape
    tile = pl.BlockSpec((None, tq, LANES), lambda bi, hp, i: (bi, i, hp))
    resident = pl.BlockSpec((None, s, LANES), lambda bi, hp, i: (bi, 0, hp))
    return pl.pallas_call(
        functools.partial(_fox_prompt_kernel, tq=tq),
        grid=(b, d // LANES, s // tq),
        in_specs=[tile, resident, resident,
                  pl.BlockSpec((None, tq, N_HEADS), lambda bi, hp, i: (bi, i, 0)),
                  pl.BlockSpec((None, N_HEADS, s), lambda bi, hp, i: (bi, 0, 0))],
        out_specs=tile, out_shape=jax.ShapeDtypeStruct((b, s, d), BF16),
        compiler_params=_params("arbitrary", "arbitrary", "arbitrary"),
        name="fox_prompt")(q, kb, vb, cum, cumt)


def _head_lane_mask(d):
    r = lax.broadcasted_iota(jnp.int32, (N_HEADS, d), 0)
    lane = lax.broadcasted_iota(jnp.int32, (N_HEADS, d), 1)
    return lane // HEAD_DIM == r


def _block_diag_query(q):
    tokens, d = q.shape
    mask = _head_lane_mask(d)
    rows = [jnp.where(mask, jnp.broadcast_to(q[t:t + 1, :], (N_HEADS, d)), 0.0)
            for t in range(tokens)]
    return jnp.concatenate(rows, axis=0).astype(BF16)


def _collect_heads(acc, tokens):
    mask = _head_lane_mask(acc.shape[1])
    rows = [jnp.sum(jnp.where(mask, acc[t * N_HEADS:(t + 1) * N_HEADS, :], 0.0), axis=0, keepdims=True)
            for t in range(tokens)]
    return jnp.concatenate(rows, axis=0)


def _pad_rows(x, rows):
    return jnp.concatenate([x, jnp.zeros((rows - x.shape[0], x.shape[1]), x.dtype)], axis=0)


def _sb_decode_kernel(pt_ref, q_ref, kn_ref, vn_ref, kc_ref, vc_ref, u_ref, o_ref,
                      qbd_ref, acc_ref, carry_ref, live_ref, *, tokens, page):
    del pt_ref
    step = pl.program_id(1)
    rows = tokens * N_HEADS

    @pl.when(step == 0)
    def _():
        qbd = _block_diag_query(q_ref[...])
        qbd_ref[...] = qbd
        r = lax.broadcasted_iota(jnp.int32, (rows, page), 0)
        c = lax.broadcasted_iota(jnp.int32, (rows, page), 1)
        mask = c < r // N_HEADS
        carry, acc = _sb_tile(qbd, _pad_rows(kn_ref[...], page).astype(BF16),
                              _pad_rows(vn_ref[...], page).astype(BF16),
                              u_ref[...], jnp.zeros((rows, 1), F32),
                              jnp.zeros(acc_ref.shape, F32), mask)
        acc_ref[...] = acc
        carry_ref[...] = carry
        live_ref[0] = (jnp.max(carry) > -SB_EXIT).astype(jnp.int32)

    @pl.when(jnp.logical_and(step > 0, live_ref[0] > 0))
    def _():
        carry, acc = _sb_tile(qbd_ref[...], kc_ref[...].astype(BF16), vc_ref[...].astype(BF16),
                              u_ref[...], carry_ref[...], acc_ref[...], None)
        acc_ref[...] = acc
        carry_ref[...] = carry
        live_ref[0] = (jnp.max(carry) > -SB_EXIT).astype(jnp.int32)

    @pl.when(step == pl.num_programs(1) - 1)
    def _():
        o_ref[...] = _collect_heads(acc_ref[...], tokens)


def _page_index(n_pages):
    def index(bi, s, pt):
        return (pt[bi, n_pages - jnp.maximum(s, 1)], 0, 0)
    return index


def _sb_decode(q, kn, vn, cache_k, cache_v, page_table):
    b, tokens, d = q.shape
    n_pages = page_table.shape[1]
    page = cache_k.shape[1]
    rows = tokens * N_HEADS
    new = pl.BlockSpec((None, tokens, d), lambda bi, s, pt: (bi, 0, 0))
    paged = pl.BlockSpec((None, page, d), _page_index(n_pages))
    return pl.pallas_call(
        functools.partial(_sb_decode_kernel, tokens=tokens, page=page),
        grid_spec=pltpu.PrefetchScalarGridSpec(
            num_scalar_prefetch=1, grid=(b, n_pages + 1),
            in_specs=[new, new, new, paged, paged,
                      pl.BlockSpec((page, page), lambda bi, s, pt: (0, 0))],
            out_specs=new,
            scratch_shapes=[pltpu.VMEM((rows, d), BF16), pltpu.VMEM((rows, d), F32),
                            pltpu.VMEM((rows, 1), F32), pltpu.SMEM((1,), jnp.int32)]),
        out_shape=jax.ShapeDtypeStruct((b, tokens, d), F32),
        compiler_params=_params("arbitrary", "arbitrary"),
        name="sb_decode")(page_table, q, kn, vn, cache_k, cache_v, _suffix_ones(page))


def _logf_suffix_kernel(pt_ref, ln_ref, lc_ref, tri_ref, r_ref, carry_ref, *, page):
    del pt_ref
    step = pl.program_id(1)

    @pl.when(step == 0)
    def _():
        carry_ref[...] = jnp.zeros_like(carry_ref)

    def emit(lg):
        p1, p2, p3 = _split3(lg)
        tri = tri_ref[...]
        suf = (_dot(tri, p1) + _dot(tri, p2)) + _dot(tri, p3) + carry_ref[0:1, :]
        r_ref[...] = suf
        total = suf[0:1, :] + lg[0:1, :]
        carry_ref[...] = jnp.broadcast_to(total, carry_ref.shape)

    @pl.when(step == 0)
    def _():
        emit(_pad_rows(ln_ref[...], page))

    @pl.when(step > 0)
    def _():
        emit(lc_ref[...])


def _logf_suffix(logf_new, cache_logf, page_table):
    b, tokens, nh = logf_new.shape
    n_pages = page_table.shape[1]
    page = cache_logf.shape[1]
    strict_upper = jnp.triu(jnp.ones((page, page), BF16), k=1)
    return pl.pallas_call(
        functools.partial(_logf_suffix_kernel, page=page),
        grid_spec=pltpu.PrefetchScalarGridSpec(
            num_scalar_prefetch=1, grid=(b, n_pages + 1),
            in_specs=[pl.BlockSpec((None, tokens, nh), lambda bi, s, pt: (bi, 0, 0)),
                      pl.BlockSpec((None, page, nh), _page_index(n_pages)),
                      pl.BlockSpec((page, page), lambda bi, s, pt: (0, 0))],
            out_specs=pl.BlockSpec((None, page, nh), lambda bi, s, pt: (bi, n_pages - s, 0)),
            scratch_shapes=[pltpu.VMEM((8, nh), F32)]),
        out_shape=jax.ShapeDtypeStruct((b, (n_pages + 1) * page, nh), F32),
        compiler_params=_params("arbitrary", "arbitrary"),
        name="logf_suffix")(page_table, logf_new, cache_logf, strict_upper)


def _fox_decode_kernel(pt_ref, q_ref, kn_ref, vn_ref, kc_ref, vc_ref, rt_ref, o_ref,
                       qbd_ref, acc_ref, m_ref, l_ref, *, tokens, page):
    del pt_ref
    step = pl.program_id(1)
    rows = tokens * N_HEADS
    bias = jnp.concatenate([rt_ref[...]] * tokens, axis=0)

    def update(s, vs):
        m, l, acc = _softmax_tile(s, vs, m_ref[...], l_ref[...], acc_ref[...])
        m_ref[...] = m
        l_ref[...] = l
        acc_ref[...] = acc

    @pl.when(step == 0)
    def _():
        qbd = _block_diag_query(q_ref[...])
        qbd_ref[...] = qbd
        m_ref[...] = jnp.full(m_ref.shape, NEG_BIG, F32)
        l_ref[...] = jnp.zeros_like(l_ref)
        acc_ref[...] = jnp.zeros_like(acc_ref)
        r = lax.broadcasted_iota(jnp.int32, (rows, page), 0)
        c = lax.broadcasted_iota(jnp.int32, (rows, page), 1)
        s = _dot_nt(qbd, _pad_rows(kn_ref[...], page).astype(BF16)) + bias
        s = jnp.where(c <= r // N_HEADS, s, NEG_BIG)
        update(s, _pad_rows(vn_ref[...], page).astype(BF16))

    @pl.when(step > 0)
    def _():
        s = _dot_nt(qbd_ref[...], kc_ref[...].astype(BF16)) + bias
        update(s, vc_ref[...].astype(BF16))

    @pl.when(step == pl.num_programs(1) - 1)
    def _():
        o_ref[...] = _collect_heads(acc_ref[...] / l_ref[...], tokens)


def _fox_decode(q, kn, vn, cache_k, cache_v, suffix_t, page_table):
    b, tokens, d = q.shape
    n_pages = page_table.shape[1]
    page = cache_k.shape[1]
    rows = tokens * N_HEADS
    new = pl.BlockSpec((None, tokens, d), lambda bi, s, pt: (bi, 0, 0))
    paged = pl.BlockSpec((None, page, d), _page_index(n_pages))
    return pl.pallas_call(
        functools.partial(_fox_decode_kernel, tokens=tokens, page=page),
        grid_spec=pltpu.PrefetchScalarGridSpec(
            num_scalar_prefetch=1, grid=(b, n_pages + 1),
            in_specs=[new, new, new, paged, paged,
                      pl.BlockSpec((None, N_HEADS, page), lambda bi, s, pt: (bi, 0, n_pages - s))],
            out_specs=new,
            scratch_shapes=[pltpu.VMEM((rows, d), BF16), pltpu.VMEM((rows, d), F32),
                            pltpu.VMEM((rows, 1), F32), pltpu.VMEM((rows, 1), F32)]),
        out_shape=jax.ShapeDtypeStruct((b, tokens, d), F32),
        compiler_params=_params("arbitrary", "arbitrary"),
        name="fox_decode")(page_table, q, kn, vn, cache_k, cache_v, suffix_t)


def _top2_gates(logits, n_experts):
    lane = lax.broadcasted_iota(jnp.int32, logits.shape, 1).astype(F32)
    lg = jnp.where(lane < n_experts, logits, NEG_BIG)
    m1 = jnp.max(lg, axis=1, keepdims=True)
    i1 = jnp.min(jnp.where(lg == m1, lane, float(LANES)), axis=1, keepdims=True)
    lg2 = jnp.where(lane == i1, NEG_BIG, lg)
    m2 = jnp.max(lg2, axis=1, keepdims=True)
    i2 = jnp.min(jnp.where(lg2 == m2, lane, float(LANES)), axis=1, keepdims=True)
    e2 = jnp.exp(m2 - m1)
    w1 = 1.0 / (1.0 + e2)
    return jnp.where(lane == i1, w1, 0.0) + jnp.where(lane == i2, e2 * w1, 0.0)


def _ffn_kernel(*refs, n_experts, routed, final):
    x_ref, o_ref, wo_ref, g_ref = refs[:4]
    refs = refs[4:]
    if routed:
        router_ref = refs[0]
        refs = refs[1:]
    wg_ref, wu_ref, wd_ref = refs[:3]
    refs = refs[3:]
    if final:
        gf_ref = refs[0]
        refs = refs[1:]
    out_ref, x1_ref, hb_ref, ye_ref, acc_ref = refs[:5]
    if routed:
        gate_ref = refs[5]
    e = pl.program_id(1)
    f = pl.program_id(2)
    last_f = f == pl.num_programs(2) - 1

    @pl.when(jnp.logical_and(e == 0, f == 0))
    def _():
        x1 = x_ref[...] + _dot(o_ref[...].astype(BF16), wo_ref[...])
        x1_ref[...] = x1
        h = _rmsnorm(x1, g_ref[...])
        hb_ref[...] = h.astype(BF16)
        acc_ref[...] = jnp.zeros_like(acc_ref)
        if routed:
            logits = None
            r1, r2, r3 = _split3(router_ref[...])
            h1, h2, h3 = _split3(h)
            for a, b in ((h3, r1), (h2, r2), (h1, r3), (h2, r1), (h1, r2), (h1, r1)):
                t = _dot(a, b)
                logits = t if logits is None else logits + t
            gate_ref[...] = _top2_gates(logits, n_experts)

    @pl.when(f == 0)
    def _():
        ye_ref[...] = jnp.zeros_like(ye_ref)

    hb = hb_ref[...]
    g = _dot(hb, wg_ref[...])
    u = _dot(hb, wu_ref[...])
    act = (g * jax.nn.sigmoid(g)) * u
    ye_ref[...] += _dot(act.astype(BF16), wd_ref[...])

    @pl.when(last_f)
    def _():
        if routed:
            gate = gate_ref[...]
            lane = lax.broadcasted_iota(jnp.int32, gate.shape, 1)
            gcol = jnp.sum(jnp.where(lane == e, gate, 0.0), axis=1, keepdims=True)
            acc_ref[...] += gcol * ye_ref[...]
        else:
            acc_ref[...] += ye_ref[...]

    @pl.when(jnp.logical_and(e == pl.num_programs(1) - 1, last_f))
    def _():
        y = x1_ref[...] + acc_ref[...]
        if final:
            y = _rmsnorm(y, gf_ref[...])
        out_ref[...] = y


def _ffn(x, o, wo, g, wg, wu, wd, router=None, g_final=None, *, tm, tf):
    t, d = x.shape
    n_experts, _, dff = wg.shape
    routed = router is not None
    final = g_final is not None
    row = lambda i, e, f: (i, 0)
    const = lambda i, e, f: (0, 0)
    in_specs = [pl.BlockSpec((tm, d), row), pl.BlockSpec((tm, d), row),
                pl.BlockSpec((d, d), const), pl.BlockSpec((1, d), const)]
    args = [x, o, wo, g.reshape(1, d)]
    if routed:
        in_specs.append(pl.BlockSpec((d, LANES), const))
        args.append(router)
    in_specs += [pl.BlockSpec((None, d, tf), lambda i, e, f: (e, 0, f)),
                 pl.BlockSpec((None, d, tf), lambda i, e, f: (e, 0, f)),
                 pl.BlockSpec((None, tf, d), lambda i, e, f: (e, f, 0))]
    args += [wg, wu, wd]
    if final:
        in_specs.append(pl.BlockSpec((1, d), const))
        args.append(g_final.reshape(1, d))
    scratch = [pltpu.VMEM((tm, d), F32), pltpu.VMEM((tm, d), BF16),
               pltpu.VMEM((tm, d), F32), pltpu.VMEM((tm, d), F32)]
    if routed:
        scratch.append(pltpu.VMEM((tm, LANES), F32))
    return pl.pallas_call(
        functools.partial(_ffn_kernel, n_experts=n_experts, routed=routed, final=final),
        grid=(t // tm, n_experts, dff // tf), in_specs=in_specs,
        out_specs=pl.BlockSpec((tm, d), row), out_shape=jax.ShapeDtypeStruct((t, d), F32),
        scratch_shapes=scratch, compiler_params=_params("arbitrary", "arbitrary", "arbitrary"),
        name="moe_ffn" if routed else "dense_ffn")(*args)


def _tile(n, want):
    while n % want:
        want //= 2
    return want


def kernel(x_prompt, x_sample, cache_sb_k, cache_sb_v, cache_fox_k, cache_fox_v, cache_fox_logf,
           page_table, ln_attn0, w_in0, w_o0, ln_ffn0, w_gate0, w_up0, w_down0, ln_attn1, w_in1,
           b_f1, w_o1, ln_ffn1, router1, w_gate1, w_up1, w_down1, ln_final):
    b, s, d = x_prompt.shape
    db, ds, _ = x_sample.shape
    n_pool, page = cache_sb_k.shape[:2]
    heads = (N_HEADS, HEAD_DIM)
    bf = lambda w: w.astype(BF16)

    w_in0b, w_o0b, w_o1b = bf(w_in0), bf(w_o0), bf(w_o1)
    w_in1b = bf(w_in1[:, :3 * d])
    wf = jnp.pad(bf(w_in1[:, 3 * d:]), ((0, 0), (0, LANES - N_HEADS)))
    bfp = jnp.pad(b_f1, (0, LANES - N_HEADS)).reshape(1, LANES)
    ffn0 = (bf(w_gate0)[None], bf(w_up0)[None], bf(w_down0)[None])
    ffn1 = (bf(w_gate1), bf(w_up1), bf(w_down1))
    router = jnp.pad(router1, ((0, 0), (0, LANES - router1.shape[1])))
    paged = lambda c: c.reshape(n_pool, page, d)

    tp, ts = b * s, db * ds
    tm_p, tm_s = _tile(s, 256), _tile(ts, 256)
    tq = _tile(s, 256)
    tf = _tile(w_gate0.shape[1], 512)
    tmf_p, tmf_s = _tile(tp, 512), _tile(ts, 256)
    xp, xs = x_prompt.reshape(tp, d), x_sample.reshape(ts, d)

    qp, k0p, v0p, kbp, vbp = _norm_proj(xp, ln_attn0, w_in0b, tm=tm_p)
    op = _sb_prompt(qp.reshape(b, s, d), kbp.reshape(b, s, d), vbp.reshape(b, s, d), tq=tq)
    xp = _ffn(xp, op.reshape(tp, d), w_o0b, ln_ffn0, *ffn0, tm=tmf_p, tf=tf)

    seq = lambda a: a.reshape(db, ds, d)
    qs, k0s, v0s, _, _ = _norm_proj(xs, ln_attn0, w_in0b, tm=tm_s)
    os_ = _sb_decode(seq(qs.astype(F32)), seq(k0s), seq(v0s),
                     paged(cache_sb_k), paged(cache_sb_v), page_table)
    xs = _ffn(xs, os_.reshape(ts, d), w_o0b, ln_ffn0, *ffn0, tm=tmf_s, tf=tf)

    qp, k1p, v1p, kbp, vbp, lfp, cum, cumt = _norm_proj(xp, ln_attn1, w_in1b, wf, bfp,
                                                        tm=tm_p, seq_len=s)
    op = _fox_prompt(qp.reshape(b, s, d), kbp.reshape(b, s, d), vbp.reshape(b, s, d),
                     cum.reshape(b, s, N_HEADS), cumt, tq=tq)
    yp = _ffn(xp, op.reshape(tp, d), w_o1b, ln_ffn1, *ffn1, router, ln_final, tm=tmf_p, tf=tf)

    qs, k1s, v1s, _, _, lfs = _norm_proj(xs, ln_attn1, w_in1b, wf, bfp, tm=tm_s)
    lfs = lfs.reshape(db, ds, N_HEADS)
    suffix = _logf_suffix(lfs, cache_fox_logf, page_table)
    os_ = _fox_decode(seq(qs.astype(F32)), seq(k1s), seq(v1s),
                      paged(cache_fox_k), paged(cache_fox_v), jnp.swapaxes(suffix, 1, 2), page_table)
    ys = _ffn(xs, os_.reshape(ts, d), w_o1b, ln_ffn1, *ffn1, router, ln_final, tm=tmf_s, tf=tf)

    hp = lambda a: a.reshape((b, s) + heads)
    hs = lambda a: a.reshape((db, ds) + heads)
    return (yp.reshape(b, s, d), ys.reshape(db, ds, d), hp(k0p), hp(v0p), hp(k1p), hp(v1p),
            lfp.reshape(b, s, N_HEADS), hs(k0s), hs(v0s), hs(k1s), hs(v1s), lfs)
```

```python
import functools

import numpy as np
import jax
import jax.numpy as jnp
from jax import lax
from jax.experimental import pallas as pl
from jax.experimental.pallas import tpu as pltpu

N_HEADS = 16
HEAD_DIM = 64
LANES = 128
HEAD_PAD = 128
AUG = HEAD_DIM
RMS_EPS = 1e-6
NEG_BIG = -1e30
SB_EXIT = 90.0
VMEM_LIMIT = 56 * 1024 * 1024

F32 = jnp.float32
BF16 = jnp.bfloat16

_NT = (((1,), (1,)), ((), ()))


def _dot(a, b):
    return jnp.dot(a, b, preferred_element_type=F32)


def _dot_nt(a, b):
    return lax.dot_general(a, b, _NT, preferred_element_type=F32)


def _split3(x):
    p1 = x.astype(BF16)
    r1 = x - p1.astype(F32)
    p2 = r1.astype(BF16)
    r2 = r1 - p2.astype(F32)
    return p1, p2, r2.astype(BF16)


def _dot3(a, pieces):
    return (_dot(a, pieces[0]) + _dot(a, pieces[1])) + _dot(a, pieces[2])


def _dot3_right(pieces, b):
    return (_dot(pieces[0], b) + _dot(pieces[1], b)) + _dot(pieces[2], b)


def _softplus(z):
    return jnp.maximum(z, 0.0) + jnp.log1p(jnp.exp(-jnp.abs(z)))


def _log_sigmoid(z):
    return jnp.minimum(z, 0.0) - jnp.log1p(jnp.exp(-jnp.abs(z)))


def _rmsnorm(x, g):
    ms = jnp.mean(x * x, axis=-1, keepdims=True)
    return (x * lax.rsqrt(ms + RMS_EPS)) * g


def _lane_tile(x, n):
    return x if n == 1 else jnp.concatenate([x] * n, axis=1)


def _params(*sem):
    return pltpu.CompilerParams(dimension_semantics=sem, vmem_limit_bytes=VMEM_LIMIT)


def _norm_proj_rows_kernel(*refs, d, gate):
    x_ref, g_ref, w_ref = refs[:3]
    refs = refs[3:]
    if gate:
        wf_ref, bf_ref = refs[:2]
        refs = refs[2:]
    q_ref, k_ref, v_ref = refs[:3]
    hb = _rmsnorm(x_ref[...], g_ref[...]).astype(BF16)
    y = _dot(hb, w_ref[...])
    q_ref[...] = y[:, :d] * (HEAD_DIM ** -0.5)
    k_ref[...] = y[:, d:2 * d]
    v_ref[...] = y[:, 2 * d:]
    if gate:
        lg = _dot(hb, wf_ref[...]) + bf_ref[...]
        refs[3][...] = _log_sigmoid(lg)[:, :N_HEADS]


def _norm_proj_rows(x, g, w, wf=None, bf=None, *, tm):
    t, d = x.shape
    gate = wf is not None
    row = lambda i: (i, 0)
    const = lambda i: (0, 0)
    in_specs = [pl.BlockSpec((tm, d), row), pl.BlockSpec((1, d), const),
                pl.BlockSpec((d, 3 * d), const)]
    args = [x, g.reshape(1, d), w]
    out_shape = [jax.ShapeDtypeStruct((t, d), F32)] * 3
    out_specs = [pl.BlockSpec((tm, d), row)] * 3
    if gate:
        in_specs += [pl.BlockSpec((d, LANES), const), pl.BlockSpec((1, LANES), const)]
        args += [wf, bf]
        out_shape.append(jax.ShapeDtypeStruct((t, N_HEADS), F32))
        out_specs.append(pl.BlockSpec((tm, N_HEADS), row))
    return pl.pallas_call(
        functools.partial(_norm_proj_rows_kernel, d=d, gate=gate),
        grid=(t // tm,), in_specs=in_specs, out_specs=out_specs, out_shape=out_shape,
        compiler_params=_params("arbitrary"),
        name="norm_proj_rows_gate" if gate else "norm_proj_rows")(*args)


def _aug_constants(tm):
    width = N_HEADS * HEAD_PAD
    place_q = np.zeros((3, LANES, width), np.float32)
    place_k = np.zeros((3, N_HEADS * 16, N_HEADS), np.float32)
    q_ones = np.zeros((1, width), np.float32)
    k_ones = np.zeros((N_HEADS * 16, LANES), np.float32)
    for h in range(N_HEADS):
        for r in range(3):
            place_q[r, h, h * HEAD_PAD + AUG + r] = 1.0
            place_k[r, h * 16 + 3 + r, h] = -1.0
            q_ones[0, h * HEAD_PAD + AUG + 3 + r] = 1.0
            k_ones[h * 16 + r, :] = 1.0
    tri = np.tril(np.ones((tm, tm), np.float32))
    return (jnp.asarray(place_q, BF16), jnp.asarray(place_k, BF16), jnp.asarray(q_ones),
            jnp.asarray(k_ones), jnp.asarray(tri, BF16), jnp.asarray(tri.T, BF16))


def _norm_proj_t_kernel(*refs, d, gate, tm, tiles_per_seq):
    x_ref, g_ref, wq_ref, wkt_ref, wvt_ref = refs[:5]
    refs = refs[5:]
    if gate:
        (wf_ref, bf_ref, wft_ref, bft_ref, pq_ref, pk_ref, qones_ref, kones_ref,
         tri_ref, trit_ref) = refs[:10]
        refs = refs[10:]
    q_ref, kt_ref, vt_ref, ka_ref, va_ref = refs[:5]
    refs = refs[5:]

    hb = _rmsnorm(x_ref[...], g_ref[...]).astype(BF16)
    q = _dot(hb, wq_ref[...]) * (HEAD_DIM ** -0.5)
    kt = _dot_nt(wkt_ref[...], hb)
    vt = _dot_nt(wvt_ref[...], hb)
    kt_ref[...] = kt
    vt_ref[...] = vt
    ktb = kt.astype(BF16)
    vtb = vt.astype(BF16)
    zeros = jnp.zeros((HEAD_PAD - HEAD_DIM - 16, tm), BF16)
    ones_row = (lax.broadcasted_iota(jnp.int32, (16, tm), 0) == 0).astype(BF16)

    if gate:
        logft_ref, carry_ref, carryt_ref = refs[:3]

        @pl.when(pl.program_id(0) % tiles_per_seq == 0)
        def _():
            carry_ref[...] = jnp.zeros_like(carry_ref)
            carryt_ref[...] = jnp.zeros_like(carryt_ref)

        logf = _log_sigmoid(_dot(hb, wf_ref[...]) + bf_ref[...])
        cum = _dot3(tri_ref[...], _split3(logf)) + carry_ref[0:1, :]
        carry_ref[...] = jnp.broadcast_to(cum[-1:, :], carry_ref.shape)
        c1, c2, c3 = _split3(cum)
        q = q + ((_dot(c1, pq_ref[0]) + _dot(c2, pq_ref[1])) + _dot(c3, pq_ref[2])) + qones_ref[...]
        logft = _log_sigmoid(_dot_nt(wft_ref[...], hb) + bft_ref[:, 0:1])
        logft_ref[...] = logft
        cumt = _dot3_right(_split3(logft), trit_ref[...]) + carryt_ref[:, 0:1]
        carryt_ref[...] = jnp.broadcast_to(cumt[:, -1:], carryt_ref.shape)
        t1, t2, t3 = _split3(cumt)
        kaug = ((_dot(pk_ref[0], t1) + _dot(pk_ref[1], t2)) + _dot(pk_ref[2], t3)
                + _lane_tile(kones_ref[...], tm // LANES)).astype(BF16)

    q_ref[...] = q.astype(BF16)
    for h in range(N_HEADS):
        lo, base = h * HEAD_DIM, h * HEAD_PAD
        ka_ref[base:base + HEAD_DIM, :] = ktb[lo:lo + HEAD_DIM, :]
        va_ref[base:base + HEAD_DIM, :] = vtb[lo:lo + HEAD_DIM, :]
        ka_ref[base + HEAD_DIM:base + HEAD_DIM + 16, :] = (
            kaug[h * 16:(h + 1) * 16, :] if gate else jnp.zeros((16, tm), BF16))
        va_ref[base + HEAD_DIM:base + HEAD_DIM + 16, :] = ones_row
        ka_ref[base + HEAD_DIM + 16:base + HEAD_PAD, :] = zeros
        va_ref[base + HEAD_DIM + 16:base + HEAD_PAD, :] = zeros


def _norm_proj_t(x, g, wq_pad, wkt, wvt, gate_args=None, *, tm, seq_len):
    t, d = x.shape
    nb = t // seq_len
    tiles_per_seq = seq_len // tm
    width = N_HEADS * HEAD_PAD
    gate = gate_args is not None
    row = lambda i: (i, 0)
    const = lambda i: (0, 0)
    col = lambda i: (i // tiles_per_seq, 0, i % tiles_per_seq)
    in_specs = [pl.BlockSpec((tm, d), row), pl.BlockSpec((1, d), const),
                pl.BlockSpec((d, width), const), pl.BlockSpec((d, d), const),
                pl.BlockSpec((d, d), const)]
    args = [x, g.reshape(1, d), wq_pad, wkt, wvt]
    out_shape = [jax.ShapeDtypeStruct((t, width), BF16),
                 jax.ShapeDtypeStruct((nb, d, seq_len), F32),
                 jax.ShapeDtypeStruct((nb, d, seq_len), F32),
                 jax.ShapeDtypeStruct((nb, width, seq_len), BF16),
                 jax.ShapeDtypeStruct((nb, width, seq_len), BF16)]
    out_specs = [pl.BlockSpec((tm, width), row), pl.BlockSpec((None, d, tm), col),
                 pl.BlockSpec((None, d, tm), col), pl.BlockSpec((None, width, tm), col),
                 pl.BlockSpec((None, width, tm), col)]
    scratch = []
    if gate:
        wf, bf, wft, bft = gate_args
        consts = _aug_constants(tm)
        args += [wf, bf, wft, bft, *consts]
        in_specs += [pl.BlockSpec((d, LANES), const), pl.BlockSpec((1, LANES), const),
                     pl.BlockSpec((N_HEADS, d), const), pl.BlockSpec((N_HEADS, LANES), const),
                     pl.BlockSpec(consts[0].shape, lambda i: (0, 0, 0)),
                     pl.BlockSpec(consts[1].shape, lambda i: (0, 0, 0)),
                     pl.BlockSpec(consts[2].shape, const), pl.BlockSpec(consts[3].shape, const),
                     pl.BlockSpec((tm, tm), const), pl.BlockSpec((tm, tm), const)]
        out_shape.append(jax.ShapeDtypeStruct((nb, N_HEADS, seq_len), F32))
        out_specs.append(pl.BlockSpec((None, N_HEADS, tm), col))
        scratch = [pltpu.VMEM((8, LANES), F32), pltpu.VMEM((N_HEADS, LANES), F32)]
    return pl.pallas_call(
        functools.partial(_norm_proj_t_kernel, d=d, gate=gate, tm=tm, tiles_per_seq=tiles_per_seq),
        grid=(t // tm,), in_specs=in_specs, out_specs=out_specs, out_shape=out_shape,
        scratch_shapes=scratch, compiler_params=_params("arbitrary"),
        name="norm_proj_t_gate" if gate else "norm_proj_t")(*args)


def _sb_tile(q, kt, vt, u, ones, carry, acc, mask, token_major=False):
    keys = u.shape[0]
    s = _dot_nt(q, kt) if token_major else _dot(q, kt)
    lk = -_softplus(s)
    if mask is not None:
        lk = jnp.where(mask, lk, 0.0)
    hi = lk.astype(BF16)
    lo = (lk - hi.astype(F32)).astype(BF16)
    suffix = _dot(hi, u) + _dot(lo, u)
    a = jnp.exp(s + suffix + _lane_tile(carry, keys // LANES))
    if mask is not None:
        a = jnp.where(mask, a, 0.0)
    ab = a.astype(BF16)
    acc = acc + (_dot(ab, vt) if token_major else _dot_nt(ab, vt))
    return carry + (_dot(hi, ones) + _dot(lo, ones)), acc


def _sb_prompt_kernel(q_ref, k_ref, v_ref, u_ref, ones_ref, o_ref, *, tq, group):
    i = pl.program_id(2)
    u = u_ref[...]
    ones = ones_ref[...]
    row = lax.broadcasted_iota(jnp.int32, (tq, tq), 0)
    col = lax.broadcasted_iota(jnp.int32, (tq, tq), 1)
    head = lambda g: slice(g * HEAD_PAD, (g + 1) * HEAD_PAD)

    def tile(j, state, mask):
        start = pl.multiple_of(j * tq, tq)
        return tuple(
            _sb_tile(q_ref[:, head(g)], k_ref[head(g), pl.ds(start, tq)],
                     v_ref[head(g), pl.ds(start, tq)], u, ones, *state[g], mask)
            for g in range(group))

    def live(state):
        top = state[0][0]
        for carry, _ in state[1:]:
            top = jnp.maximum(top, carry)
        return (jnp.max(top) > -SB_EXIT).astype(jnp.int32)

    zero = jnp.zeros((tq, LANES), F32)
    state = tile(i, ((zero, zero),) * group, col < row)

    def body(st):
        j, _, state = st
        state = tile(j, state, None)
        return j - 1, live(state), state

    state = lax.while_loop(lambda st: jnp.logical_and(st[0] >= 0, st[1] > 0), body,
                           (i - 1, live(state), state))[2]
    for g in range(group):
        o_ref[:, head(g)] = state[g][1].astype(BF16)


def _suffix_ones(n):
    return jnp.asarray(np.tril(np.ones((n, n), np.float32)), BF16)


def _attn_specs(s, tq, group):
    tile = pl.BlockSpec((None, tq, group * HEAD_PAD), lambda bi, h, i: (bi, i, h))
    resident = pl.BlockSpec((None, group * HEAD_PAD, s), lambda bi, h, i: (bi, h, 0))
    return tile, resident


def _sb_prompt(q, kat, vat, *, tq, group):
    b, s, width = q.shape
    tile, resident = _attn_specs(s, tq, group)
    const = lambda bi, h, i: (0, 0)
    return pl.pallas_call(
        functools.partial(_sb_prompt_kernel, tq=tq, group=group),
        grid=(b, N_HEADS // group, s // tq),
        in_specs=[tile, resident, resident, pl.BlockSpec((tq, tq), const),
                  pl.BlockSpec((tq, LANES), const)],
        out_specs=tile, out_shape=jax.ShapeDtypeStruct((b, s, width), BF16),
        compiler_params=_params("arbitrary", "arbitrary", "arbitrary"),
        name="sb_prompt")(q, kat, vat, _suffix_ones(tq), jnp.ones((tq, LANES), BF16))


def _fox_prompt_kernel(q_ref, k_ref, v_ref, o_ref, m_ref, acc_ref, *, tq, group):
    i = pl.program_id(2)
    row = lax.broadcasted_iota(jnp.int32, (tq, tq), 0)
    col = lax.broadcasted_iota(jnp.int32, (tq, tq), 1)
    m_ref[...] = jnp.full(m_ref.shape, NEG_BIG, F32)
    acc_ref[...] = jnp.zeros_like(acc_ref)
    head = lambda g: slice(g * HEAD_PAD, (g + 1) * HEAD_PAD)

    def tile(j, mask):
        start = pl.multiple_of(j * tq, tq)
        for g in range(group):
            s = _dot(q_ref[:, head(g)], k_ref[head(g), pl.ds(start, tq)])
            if mask is not None:
                s = jnp.where(mask, s, NEG_BIG)
            m_prev = m_ref[g]
            m_next = jnp.maximum(m_prev, jnp.max(s, axis=1, keepdims=True))
            p = jnp.exp(s - _lane_tile(m_next, tq // LANES))
            acc_ref[g] = (acc_ref[g] * jnp.exp(m_prev - m_next)
                          + _dot_nt(p.astype(BF16), v_ref[head(g), pl.ds(start, tq)]))
            m_ref[g] = m_next

    tile(i, col <= row)

    @pl.loop(0, i)
    def _(j):
        tile(j, None)

    for g in range(group):
        acc = acc_ref[g]
        lane = lax.broadcasted_iota(jnp.int32, acc.shape, 1)
        denom = jnp.sum(jnp.where(lane == AUG, acc, 0.0), axis=1, keepdims=True)
        o_ref[:, head(g)] = (acc / denom).astype(BF16)


def _fox_prompt(q, kat, vat, *, tq, group):
    b, s, width = q.shape
    tile, resident = _attn_specs(s, tq, group)
    return pl.pallas_call(
        functools.partial(_fox_prompt_kernel, tq=tq, group=group),
        grid=(b, N_HEADS // group, s // tq),
        in_specs=[tile, resident, resident],
        out_specs=tile, out_shape=jax.ShapeDtypeStruct((b, s, width), BF16),
        scratch_shapes=[pltpu.VMEM((group, tq, LANES), F32), pltpu.VMEM((group, tq, HEAD_PAD), F32)],
        compiler_params=_params("arbitrary", "arbitrary", "arbitrary"),
        name="fox_prompt")(q, kat, vat)


def _head_lane_mask(d):
    r = lax.broadcasted_iota(jnp.int32, (N_HEADS, d), 0)
    lane = lax.broadcasted_iota(jnp.int32, (N_HEADS, d), 1)
    return lane // HEAD_DIM == r


def _block_diag_query(q):
    tokens, d = q.shape
    mask = _head_lane_mask(d)
    rows = [jnp.where(mask, jnp.broadcast_to(q[t:t + 1, :], (N_HEADS, d)), 0.0)
            for t in range(tokens)]
    return jnp.concatenate(rows, axis=0).astype(BF16)


def _collect_heads(acc, tokens):
    mask = _head_lane_mask(acc.shape[1])
    rows = [jnp.sum(jnp.where(mask, acc[t * N_HEADS:(t + 1) * N_HEADS, :], 0.0), axis=0, keepdims=True)
            for t in range(tokens)]
    return jnp.concatenate(rows, axis=0)


def _pad_rows(x, rows):
    return jnp.concatenate([x, jnp.zeros((rows - x.shape[0], x.shape[1]), x.dtype)], axis=0)


def _new_token_mask(rows, page, strict):
    r = lax.broadcasted_iota(jnp.int32, (rows, page), 0) // N_HEADS
    c = lax.broadcasted_iota(jnp.int32, (rows, page), 1)
    return c < r if strict else c <= r


def _sb_decode_kernel(*refs, tokens, page, npp, first):
    q_ref = refs[2 - first]
    refs = refs[3 - first:]
    if first:
        kn_ref, vn_ref = refs[:2]
    else:
        acc_in_ref, carry_in_ref = refs[:2]
    refs = refs[2:]
    kc_refs, vc_refs = refs[:npp], refs[npp:2 * npp]
    u_ref, ones_ref = refs[2 * npp:2 * npp + 2]
    refs = refs[2 * npp + 2:]
    if first:
        acc_ref, carry_ref, qbd_ref, live_ref = refs
    else:
        o_ref, acc_ref, carry_ref, qbd_ref, live_ref = refs
    step = pl.program_id(1)
    rows = tokens * N_HEADS
    u = u_ref[...]
    ones = ones_ref[...]

    def is_live(carry):
        return (jnp.max(carry) > -SB_EXIT).astype(jnp.int32)

    @pl.when(step == 0)
    def _():
        qbd = _block_diag_query(q_ref[...])
        qbd_ref[...] = qbd
        if first:
            carry, acc = _sb_tile(qbd, _pad_rows(kn_ref[...], page).astype(BF16),
                                  _pad_rows(vn_ref[...], page).astype(BF16), u, ones,
                                  jnp.zeros((rows, LANES), F32), jnp.zeros(acc_ref.shape, F32),
                                  _new_token_mask(rows, page, True), token_major=True)
        else:
            carry, acc = carry_in_ref[...], acc_in_ref[...]
        acc_ref[...] = acc
        carry_ref[...] = carry
        live_ref[0] = is_live(carry)

    @pl.when(live_ref[0] > 0)
    def _():
        carry, acc = carry_ref[...], acc_ref[...]
        qbd = qbd_ref[...]
        for kc, vc in zip(kc_refs, vc_refs):
            carry, acc = _sb_tile(qbd, kc[...].astype(BF16), vc[...].astype(BF16), u, ones,
                                  carry, acc, None)
        acc_ref[...] = acc
        carry_ref[...] = carry
        live_ref[0] = is_live(carry)

    if not first:
        @pl.when(step == pl.num_programs(1) - 1)
        def _():
            o_ref[...] = _collect_heads(acc_ref[...], tokens)


def _sb_decode(q, kn, vn, cache_kt, cache_vt, page_table, *, head_pages, npp):
    b, tokens, d = q.shape
    n_pages = page_table.shape[1]
    page = cache_kt.shape[2]
    rows = tokens * N_HEADS
    u, ones = _suffix_ones(page), jnp.ones((page, LANES), BF16)
    new = pl.BlockSpec((None, tokens, d), lambda bi, s, *_: (bi, 0, 0))
    consts = [pl.BlockSpec((page, page), lambda bi, s, *_: (0, 0)),
              pl.BlockSpec((page, LANES), lambda bi, s, *_: (0, 0))]
    acc_spec = pl.BlockSpec((None, rows, d), lambda bi, s, *_: (bi, 0, 0))
    carry_spec = pl.BlockSpec((None, rows, LANES), lambda bi, s, *_: (bi, 0, 0))
    scratch = [pltpu.VMEM((rows, d), BF16), pltpu.SMEM((1,), jnp.int32)]

    def paged(first_page, gated):
        def make(i):
            def index(bi, s, pt, *live):
                p = first_page - (s * npp + i)
                if gated:
                    p = jnp.where(live[0][bi] > 0, p, first_page)
                return (pt[bi, p], 0, 0)
            return pl.BlockSpec((None, d, page), index)
        return [make(i) for i in range(npp)] * 2

    acc, carry = pl.pallas_call(
        functools.partial(_sb_decode_kernel, tokens=tokens, page=page, npp=npp, first=True),
        grid_spec=pltpu.PrefetchScalarGridSpec(
            num_scalar_prefetch=1, grid=(b, head_pages // npp),
            in_specs=[new, new, new] + paged(n_pages - 1, False) + consts,
            out_specs=[acc_spec, carry_spec], scratch_shapes=scratch),
        out_shape=[jax.ShapeDtypeStruct((b, rows, d), F32), jax.ShapeDtypeStruct((b, rows, LANES), F32)],
        compiler_params=_params("arbitrary", "arbitrary"),
        name="sb_decode_head")(page_table, q, kn, vn, *([cache_kt] * npp), *([cache_vt] * npp), u, ones)
    live = (jnp.max(carry, axis=(1, 2)) > -SB_EXIT).astype(jnp.int32)
    return pl.pallas_call(
        functools.partial(_sb_decode_kernel, tokens=tokens, page=page, npp=npp, first=False),
        grid_spec=pltpu.PrefetchScalarGridSpec(
            num_scalar_prefetch=2, grid=(b, (n_pages - head_pages) // npp),
            in_specs=[new, acc_spec, carry_spec] + paged(n_pages - 1 - head_pages, True) + consts,
            out_specs=new,
            scratch_shapes=[pltpu.VMEM((rows, d), F32), pltpu.VMEM((rows, LANES), F32)] + scratch),
        out_shape=jax.ShapeDtypeStruct((b, tokens, d), F32),
        compiler_params=_params("arbitrary", "arbitrary"),
        name="sb_decode_tail")(page_table, live, q, acc, carry, *([cache_kt] * npp),
                               *([cache_vt] * npp), u, ones)


def _fox_decode_kernel(*refs, tokens, page, npp):
    q_ref, kn_ref, vn_ref, lfn_ref = refs[1:5]
    refs = refs[5:]
    kc_refs, vc_refs, lc_refs = refs[:npp], refs[npp:2 * npp], refs[2 * npp:3 * npp]
    us_ref, ones_ref, o_ref, qbd_ref, acc_ref, m_ref, l_ref, fcarry_ref = refs[3 * npp:]
    step = pl.program_id(1)
    rows = tokens * N_HEADS
    us = us_ref[...]
    ones = ones_ref[...]

    def bias_of(lg):
        pieces = _split3(lg)
        suffix = _dot3_right(pieces, us) + fcarry_ref[...]
        fcarry_ref[...] += _dot3_right(pieces, ones)
        return jnp.concatenate([suffix] * tokens, axis=0)

    def update(s_list, pv):
        m_prev = m_ref[...]
        m_next = m_prev
        for s in s_list:
            m_next = jnp.maximum(m_next, jnp.max(s, axis=1, keepdims=True))
        alpha = jnp.exp(m_prev - m_next)
        acc = acc_ref[...] * _lane_tile(alpha, acc_ref.shape[1] // LANES)
        l = l_ref[...] * alpha
        for s, v in zip(s_list, pv):
            p = jnp.exp(s - m_next).astype(BF16)
            acc = acc + v(p)
            l = l + _dot(p, ones)
        acc_ref[...] = acc
        l_ref[...] = l
        m_ref[...] = m_next

    @pl.when(step == 0)
    def _():
        qbd = _block_diag_query(q_ref[...])
        qbd_ref[...] = qbd
        m_ref[...] = jnp.full(m_ref.shape, NEG_BIG, F32)
        l_ref[...] = jnp.zeros_like(l_ref)
        acc_ref[...] = jnp.zeros_like(acc_ref)
        fcarry_ref[...] = jnp.zeros_like(fcarry_ref)
        lgn = _pad_rows(jnp.concatenate(
            [lfn_ref[...], jnp.zeros((tokens, LANES - N_HEADS), F32)], axis=1), page).T[:N_HEADS, :]
        kn = _pad_rows(kn_ref[...], page).astype(BF16)
        vn = _pad_rows(vn_ref[...], page).astype(BF16)
        s = _dot_nt(qbd, kn) + bias_of(lgn)
        s = jnp.where(_new_token_mask(rows, page, False), s, NEG_BIG)
        update([s], [lambda p: _dot(p, vn)])

    qbd = qbd_ref[...]
    s_list, pv = [], []
    for kc, vc, lc in zip(kc_refs, vc_refs, lc_refs):
        s_list.append(_dot(qbd, kc[...].astype(BF16)) + bias_of(lc[...]))
        pv.append(lambda p, vc=vc: _dot_nt(p, vc[...].astype(BF16)))
    update(s_list, pv)

    @pl.when(step == pl.num_programs(1) - 1)
    def _():
        out = acc_ref[...] / _lane_tile(l_ref[...], acc_ref.shape[1] // LANES)
        o_ref[...] = _collect_heads(out, tokens)


def _fox_decode(q, kn, vn, lfn, cache_kt, cache_vt, cache_lt, page_table, *, npp):
    b, tokens, d = q.shape
    n_pages = page_table.shape[1]
    page = cache_kt.shape[2]
    rows = tokens * N_HEADS
    new = pl.BlockSpec((None, tokens, d), lambda bi, s, pt: (bi, 0, 0))

    def paged(width):
        def make(i):
            return pl.BlockSpec((None, width, page),
                                lambda bi, s, pt: (pt[bi, n_pages - 1 - (s * npp + i)], 0, 0))
        return [make(i) for i in range(npp)]

    strict = jnp.asarray(np.triu(np.ones((page, page), np.float32), k=1).T, BF16)
    return pl.pallas_call(
        functools.partial(_fox_decode_kernel, tokens=tokens, page=page, npp=npp),
        grid_spec=pltpu.PrefetchScalarGridSpec(
            num_scalar_prefetch=1, grid=(b, n_pages // npp),
            in_specs=[new, new, new, pl.BlockSpec((None, tokens, N_HEADS), lambda bi, s, pt: (bi, 0, 0))]
            + paged(d) + paged(d) + paged(N_HEADS)
            + [pl.BlockSpec((page, page), lambda bi, s, pt: (0, 0)),
               pl.BlockSpec((page, LANES), lambda bi, s, pt: (0, 0))],
            out_specs=new,
            scratch_shapes=[pltpu.VMEM((rows, d), BF16), pltpu.VMEM((rows, d), F32),
                            pltpu.VMEM((rows, LANES), F32), pltpu.VMEM((rows, LANES), F32),
                            pltpu.VMEM((N_HEADS, LANES), F32)]),
        out_shape=jax.ShapeDtypeStruct((b, tokens, d), F32),
        compiler_params=_params("arbitrary", "arbitrary"),
        name="fox_decode")(page_table, q, kn, vn, lfn, *([cache_kt] * npp), *([cache_vt] * npp),
                           *([cache_lt] * npp), strict, jnp.ones((page, LANES), BF16))


def _top2_gates(logits, n_experts):
    lane = lax.broadcasted_iota(jnp.int32, logits.shape, 1).astype(F32)
    lg = jnp.where(lane < n_experts, logits, NEG_BIG)
    m1 = jnp.max(lg, axis=1, keepdims=True)
    i1 = jnp.min(jnp.where(lg == m1, lane, float(LANES)), axis=1, keepdims=True)
    lg2 = jnp.where(lane == i1, NEG_BIG, lg)
    m2 = jnp.max(lg2, axis=1, keepdims=True)
    i2 = jnp.min(jnp.where(lg2 == m2, lane, float(LANES)), axis=1, keepdims=True)
    e2 = jnp.exp(m2 - m1)
    w1 = 1.0 / (1.0 + e2)
    gate = jnp.where(lane == i1, w1, 0.0) + jnp.where(lane == i2, e2 * w1, 0.0)
    return gate, jnp.logical_or(lane == i1, lane == i2)


def _router_logits(h, router):
    logits = None
    r1, r2, r3 = _split3(router)
    h1, h2, h3 = _split3(h)
    for a, b in ((h3, r1), (h2, r2), (h1, r3), (h2, r1), (h1, r2), (h1, r1)):
        t = _dot(a, b)
        logits = t if logits is None else logits + t
    return logits


def _swiglu_partial(hb, wg_ref, wu_ref, wd_ref):
    g = _dot(hb, wg_ref[...])
    u = _dot(hb, wu_ref[...])
    return _dot(((g * jax.nn.sigmoid(g)) * u).astype(BF16), wd_ref[...])


def _ffn_kernel(x_ref, o_ref, wo_ref, g_ref, wg_ref, wu_ref, wd_ref, out_ref, x1_ref, hb_ref, acc_ref):
    f = pl.program_id(1)

    @pl.when(f == 0)
    def _():
        x1 = x_ref[...] + _dot(o_ref[...].astype(BF16), wo_ref[...])
        x1_ref[...] = x1
        hb_ref[...] = _rmsnorm(x1, g_ref[...]).astype(BF16)
        acc_ref[...] = jnp.zeros_like(acc_ref)

    acc_ref[...] += _swiglu_partial(hb_ref[...], wg_ref, wu_ref, wd_ref)

    @pl.when(f == pl.num_programs(1) - 1)
    def _():
        out_ref[...] = x1_ref[...] + acc_ref[...]


def _ffn(x, o, wo, g, wg, wu, wd, *, tm, tf):
    t, d = x.shape
    dff = wg.shape[1]
    row = lambda i, f: (i, 0)
    const = lambda i, f: (0, 0)
    return pl.pallas_call(
        _ffn_kernel, grid=(t // tm, dff // tf),
        in_specs=[pl.BlockSpec((tm, d), row), pl.BlockSpec((tm, o.shape[1]), row),
                  pl.BlockSpec(wo.shape, const), pl.BlockSpec((1, d), const),
                  pl.BlockSpec((d, tf), lambda i, f: (0, f)), pl.BlockSpec((d, tf), lambda i, f: (0, f)),
                  pl.BlockSpec((tf, d), lambda i, f: (f, 0))],
        out_specs=pl.BlockSpec((tm, d), row), out_shape=jax.ShapeDtypeStruct((t, d), F32),
        scratch_shapes=[pltpu.VMEM((tm, d), F32), pltpu.VMEM((tm, d), BF16), pltpu.VMEM((tm, d), F32)],
        compiler_params=_params("arbitrary", "arbitrary"),
        name="dense_ffn")(x, o, wo, g.reshape(1, d), wg, wu, wd)


SEG_ALIGN = 16
MOE_CHUNK = 256
MOE_ROWS = 512


def _moe_pre_kernel(x_ref, o_ref, wo_ref, g_ref, router_ref, lower_ref, upper_ref,
                    x1_ref, hb_ref, gate_ref, rank_ref, rankt_ref, count_ref, *, n_experts):
    x1 = x_ref[...] + _dot(o_ref[...].astype(BF16), wo_ref[...])
    x1_ref[...] = x1
    h = _rmsnorm(x1, g_ref[...])
    hb_ref[...] = h.astype(BF16)
    gate, routed = _top2_gates(_router_logits(h, router_ref[...]), n_experts)
    gate_ref[...] = gate
    mask = routed.astype(F32)
    rank = _dot(lower_ref[...], mask.astype(BF16))
    rank_ref[...] = jnp.where(routed, rank, -1.0)
    mask_t = mask.T
    rank_t = _dot(mask_t.astype(BF16), upper_ref[...])
    rankt_ref[...] = jnp.where(mask_t > 0.0, rank_t, -1.0)[:8, :]
    count_ref[...] = jnp.broadcast_to(jnp.sum(mask, axis=0, keepdims=True), count_ref.shape)


def _moe_dispatch_kernel(seg_ref, hb_ref, rankt_ref, init_ref, xs_ref):
    del seg_ref, init_ref
    e = pl.program_id(1)
    n = hb_ref.shape[0]
    slot = lax.broadcasted_iota(jnp.int32, (n, n), 0).astype(F32)
    pick = (slot == rankt_ref[pl.ds(e, 1), :]).astype(BF16)
    xs_ref[...] = _dot(pick, hb_ref[...]).astype(BF16)


def _moe_expert_kernel(expert_ref, valid_ref, xs_ref, wg_ref, wu_ref, wd_ref, ys_ref, acc_ref):
    del expert_ref
    r, f = pl.program_id(0), pl.program_id(1)

    @pl.when(valid_ref[r] > 0)
    def _():
        @pl.when(f == 0)
        def _():
            acc_ref[...] = jnp.zeros_like(acc_ref)

        acc_ref[...] += _swiglu_partial(xs_ref[...], wg_ref, wu_ref, wd_ref)

        @pl.when(f == pl.num_programs(1) - 1)
        def _():
            ys_ref[...] = acc_ref[...]

    @pl.when(valid_ref[r] == 0)
    def _():
        ys_ref[...] = jnp.zeros_like(ys_ref)


def _moe_combine_kernel(seg_ref, *refs, n_experts):
    del seg_ref
    ys_refs = refs[:n_experts]
    rank_ref, gate_ref, x1_ref, gf_ref, out_ref = refs[n_experts:]
    n = rank_ref.shape[0]
    rank, gate = rank_ref[...], gate_ref[...]
    lane = lax.broadcasted_iota(jnp.int32, rank.shape, 1)
    slot = lax.broadcasted_iota(jnp.int32, (n, n), 1).astype(F32)
    y = jnp.zeros(x1_ref.shape, F32)
    for e, ys_ref in enumerate(ys_refs):
        column = lambda a: jnp.sum(jnp.where(lane == e, a, 0.0), axis=1, keepdims=True)
        pick = (slot == column(rank)).astype(BF16)
        rows = ys_ref[...]
        hi = rows.astype(BF16)
        lo = (rows - hi.astype(F32)).astype(BF16)
        y = y + column(gate) * (_dot(pick, hi) + _dot(pick, lo))
    out_ref[...] = _rmsnorm(x1_ref[...] + y, gf_ref[...])


def _moe(x, o, wo, g, router, wg, wu, wd, g_final, *, tf):
    t, d = x.shape
    n_experts, _, dff = wg.shape
    chunk = _tile(t, MOE_CHUNK)
    n_chunks = t // chunk
    row = lambda i: (i, 0)
    const = lambda i: (0, 0)
    lower = jnp.asarray(np.tril(np.ones((chunk, chunk), np.float32), k=-1), BF16)
    x1, hb, gate, rank, rank_t, counts = pl.pallas_call(
        functools.partial(_moe_pre_kernel, n_experts=n_experts),
        grid=(n_chunks,),
        in_specs=[pl.BlockSpec((chunk, d), row), pl.BlockSpec((chunk, o.shape[1]), row),
                  pl.BlockSpec(wo.shape, const), pl.BlockSpec((1, d), const),
                  pl.BlockSpec((d, LANES), const), pl.BlockSpec((chunk, chunk), const),
                  pl.BlockSpec((chunk, chunk), const)],
        out_specs=[pl.BlockSpec((chunk, d), row), pl.BlockSpec((chunk, d), row),
                   pl.BlockSpec((chunk, LANES), row), pl.BlockSpec((chunk, LANES), row),
                   pl.BlockSpec((None, 8, chunk), lambda i: (i, 0, 0)),
                   pl.BlockSpec((None, 8, LANES), lambda i: (i, 0, 0))],
        out_shape=[jax.ShapeDtypeStruct((t, d), F32), jax.ShapeDtypeStruct((t, d), BF16),
                   jax.ShapeDtypeStruct((t, LANES), F32), jax.ShapeDtypeStruct((t, LANES), F32),
                   jax.ShapeDtypeStruct((n_chunks, 8, chunk), F32),
                   jax.ShapeDtypeStruct((n_chunks, 8, LANES), F32)],
        compiler_params=_params("arbitrary"),
        name="moe_pre")(x, o, wo, g.reshape(1, d), router, lower, lower.T)

    rows_tile = MOE_ROWS
    n = counts[:, 0, :n_experts].astype(jnp.int32)
    n_pad = (n + SEG_ALIGN - 1) // SEG_ALIGN * SEG_ALIGN
    cap = (jnp.sum(n_pad, axis=0) + chunk + rows_tile - 1) // rows_tile * rows_tile
    region_end = jnp.cumsum(cap)
    region_start = region_end - cap
    seg = (region_start[None, :] + jnp.cumsum(n_pad, axis=0) - n_pad).reshape(-1)
    seg = seg // SEG_ALIGN
    total_rows = (2 * t + n_chunks * n_experts * (SEG_ALIGN - 1) + n_experts * (chunk + rows_tile - 1))
    n_tiles = -(-total_rows // rows_tile)
    tile_start = jnp.arange(n_tiles, dtype=jnp.int32) * rows_tile
    tile_valid = (tile_start < region_end[-1]).astype(jnp.int32)
    tile_expert = jnp.minimum(jnp.searchsorted(region_end, tile_start, side="right"),
                              n_experts - 1).astype(jnp.int32)

    seg_index = lambda c, e, seg_ref: (seg_ref[c * n_experts + e] * SEG_ALIGN, 0)
    xs = pl.pallas_call(
        _moe_dispatch_kernel,
        grid_spec=pltpu.PrefetchScalarGridSpec(
            num_scalar_prefetch=1, grid=(n_chunks, n_experts),
            in_specs=[pl.BlockSpec((chunk, d), lambda c, e, s: (c, 0)),
                      pl.BlockSpec((None, 8, chunk), lambda c, e, s: (c, 0, 0)),
                      pl.BlockSpec(memory_space=pl.ANY)],
            out_specs=pl.BlockSpec((pl.Element(chunk), pl.Element(d)), seg_index)),
        out_shape=jax.ShapeDtypeStruct((n_tiles * rows_tile, d), BF16),
        input_output_aliases={3: 0},
        compiler_params=_params("arbitrary", "arbitrary"),
        name="moe_dispatch")(seg, hb, rank_t, jnp.zeros((n_tiles * rows_tile, d), BF16))

    n_f = dff // tf
    f_index = lambda r, f, valid: jnp.where(valid[r] > 0, f, n_f - 1)
    ys = pl.pallas_call(
        _moe_expert_kernel,
        grid_spec=pltpu.PrefetchScalarGridSpec(
            num_scalar_prefetch=2, grid=(n_tiles, n_f),
            in_specs=[pl.BlockSpec((rows_tile, d), lambda r, f, ex, va: (r, 0)),
                      pl.BlockSpec((None, d, tf), lambda r, f, ex, va: (ex[r], 0, f_index(r, f, va))),
                      pl.BlockSpec((None, d, tf), lambda r, f, ex, va: (ex[r], 0, f_index(r, f, va))),
                      pl.BlockSpec((None, tf, d), lambda r, f, ex, va: (ex[r], f_index(r, f, va), 0))],
            out_specs=pl.BlockSpec((rows_tile, d), lambda r, f, ex, va: (r, 0)),
            scratch_shapes=[pltpu.VMEM((rows_tile, d), F32)]),
        out_shape=jax.ShapeDtypeStruct((n_tiles * rows_tile, d), F32),
        compiler_params=_params("arbitrary", "arbitrary"),
        name="moe_experts")(tile_expert, tile_valid, xs, wg, wu, wd)

    def ys_spec(e):
        return pl.BlockSpec((pl.Element(chunk), pl.Element(d)), lambda c, s: (s[c * n_experts + e] * SEG_ALIGN, 0))

    return pl.pallas_call(
        functools.partial(_moe_combine_kernel, n_experts=n_experts),
        grid_spec=pltpu.PrefetchScalarGridSpec(
            num_scalar_prefetch=1, grid=(n_chunks,),
            in_specs=[ys_spec(e) for e in range(n_experts)]
            + [pl.BlockSpec((chunk, LANES), lambda c, s: (c, 0)),
               pl.BlockSpec((chunk, LANES), lambda c, s: (c, 0)),
               pl.BlockSpec((chunk, d), lambda c, s: (c, 0)),
               pl.BlockSpec((1, d), lambda c, s: (0, 0))],
            out_specs=pl.BlockSpec((chunk, d), lambda c, s: (c, 0))),
        out_shape=jax.ShapeDtypeStruct((t, d), F32),
        compiler_params=_params("arbitrary"),
        name="moe_combine")(seg, *([ys] * n_experts), rank, gate, x1, g_final.reshape(1, d))


def _tile(n, want):
    while n % want:
        want //= 2
    return want


def _pad_heads(w, axis):
    shape = w.shape[:axis] + (N_HEADS, HEAD_DIM) + w.shape[axis + 1:]
    pad = [(0, 0)] * (len(shape))
    pad[axis + 1] = (0, HEAD_PAD - HEAD_DIM)
    padded = jnp.pad(w.reshape(shape), pad)
    return padded.reshape(w.shape[:axis] + (N_HEADS * HEAD_PAD,) + w.shape[axis + 1:])


def kernel(x_prompt, x_sample, cache_sb_k, cache_sb_v, cache_fox_k, cache_fox_v, cache_fox_logf,
           page_table, ln_attn0, w_in0, w_o0, ln_ffn0, w_gate0, w_up0, w_down0, ln_attn1, w_in1,
           b_f1, w_o1, ln_ffn1, router1, w_gate1, w_up1, w_down1, ln_final):
    b, s, d = x_prompt.shape
    db, ds, _ = x_sample.shape
    n_pool, page = cache_sb_k.shape[:2]
    n_pages = page_table.shape[1]
    bf = lambda w: w.astype(BF16)

    def attn_weights(w_in, w_o):
        w = bf(w_in[:, :3 * d])
        return (w, _pad_heads(w[:, :d], 1), w[:, d:2 * d].T, w[:, 2 * d:].T,
                bf(w_o), _pad_heads(bf(w_o), 0))

    w0_rows, wq0, wkt0, wvt0, wo0, wo0_pad = attn_weights(w_in0, w_o0)
    w1_rows, wq1, wkt1, wvt1, wo1, wo1_pad = attn_weights(w_in1, w_o1)
    wf_cols = bf(w_in1[:, 3 * d:])
    wf = jnp.pad(wf_cols, ((0, 0), (0, LANES - N_HEADS)))
    bf_row = jnp.pad(b_f1, (0, LANES - N_HEADS)).reshape(1, LANES)
    gate_args = (wf, bf_row, wf_cols.T, jnp.broadcast_to(b_f1[:, None], (N_HEADS, LANES)))
    ffn0 = (bf(w_gate0), bf(w_up0), bf(w_down0))
    ffn1 = (bf(w_gate1), bf(w_up1), bf(w_down1))
    router = jnp.pad(router1, ((0, 0), (0, LANES - router1.shape[1])))
    paged_t = lambda c: jnp.transpose(c, (0, 2, 3, 1)).reshape(n_pool, d, page)
    logf_t = jnp.transpose(cache_fox_logf, (0, 2, 1))

    tp, ts = b * s, db * ds
    tm_p, tm_s = _tile(s, 256), _tile(ts, 256)
    tq_sb, tq_fox = _tile(s, 256), _tile(s, 512)
    tf = _tile(w_gate0.shape[1], 512)
    tmf_p, tmf_s = _tile(tp, 512), _tile(ts, 256)
    npp = 4 if n_pages % 4 == 0 and n_pages >= 8 else 1
    xp, xs = x_prompt.reshape(tp, d), x_sample.reshape(ts, d)
    seq = lambda a: a.reshape(db, ds, -1)
    width = N_HEADS * HEAD_PAD
    heads_out = lambda t: jnp.transpose(t.reshape(b, N_HEADS, HEAD_DIM, s), (0, 3, 1, 2))

    qp, k0t, v0t, kat, vat = _norm_proj_t(xp, ln_attn0, wq0, wkt0, wvt0, tm=tm_p, seq_len=s)
    op = _sb_prompt(qp.reshape(b, s, width), kat, vat, tq=tq_sb, group=2)
    xp = _ffn(xp, op.reshape(tp, width), wo0_pad, ln_ffn0, *ffn0, tm=tmf_p, tf=tf)

    qs, k0s, v0s = _norm_proj_rows(xs, ln_attn0, w0_rows, tm=tm_s)
    os_ = _sb_decode(seq(qs), seq(k0s), seq(v0s), paged_t(cache_sb_k), paged_t(cache_sb_v),
                     page_table, head_pages=npp, npp=npp)
    xs = _ffn(xs, os_.reshape(ts, d), wo0, ln_ffn0, *ffn0, tm=tmf_s, tf=tf)

    qp, k1t, v1t, kat, vat, lft = _norm_proj_t(xp, ln_attn1, wq1, wkt1, wvt1, gate_args,
                                               tm=tm_p, seq_len=s)
    op = _fox_prompt(qp.reshape(b, s, width), kat, vat, tq=tq_fox, group=2)
    yp = _moe(xp, op.reshape(tp, width), wo1_pad, ln_ffn1, router, *ffn1, ln_final, tf=tf)

    qs, k1s, v1s, lfs = _norm_proj_rows(xs, ln_attn1, w1_rows, wf, bf_row, tm=tm_s)
    os_ = _fox_decode(seq(qs), seq(k1s), seq(v1s), seq(lfs), paged_t(cache_fox_k),
                      paged_t(cache_fox_v), logf_t, page_table, npp=npp)
    ys = _moe(xs, os_.reshape(ts, d), wo1, ln_ffn1, router, *ffn1, ln_final, tf=tf)

    hs = lambda a: a.reshape(db, ds, N_HEADS, HEAD_DIM)
    return (yp.reshape(b, s, d), ys.reshape(db, ds, d), heads_out(k0t), heads_out(v0t),
            heads_out(k1t), heads_out(v1t), jnp.transpose(lft, (0, 2, 1)),
            hs(k0s), hs(v0s), hs(k1s), hs(v1s), seq(lfs))
```

```python
import functools

import numpy as np
import jax
import jax.numpy as jnp
from jax import lax
from jax.experimental import pallas as pl
from jax.experimental.pallas import tpu as pltpu

N_HEADS = 16
HEAD_DIM = 64
LANES = 128
HEAD_PAD = 128
AUG = HEAD_DIM
RMS_EPS = 1e-6
NEG_BIG = -1e30
SB_EXIT = 90.0
VMEM_LIMIT = 56 * 1024 * 1024

F32 = jnp.float32
BF16 = jnp.bfloat16

_NT = (((1,), (1,)), ((), ()))


def _dot(a, b):
    return jnp.dot(a, b, preferred_element_type=F32)


def _dot_nt(a, b):
    return lax.dot_general(a, b, _NT, preferred_element_type=F32)


def _split3(x):
    p1 = x.astype(BF16)
    r1 = x - p1.astype(F32)
    p2 = r1.astype(BF16)
    r2 = r1 - p2.astype(F32)
    return p1, p2, r2.astype(BF16)


def _dot3(a, pieces):
    return (_dot(a, pieces[0]) + _dot(a, pieces[1])) + _dot(a, pieces[2])


def _dot3_right(pieces, b):
    return (_dot(pieces[0], b) + _dot(pieces[1], b)) + _dot(pieces[2], b)


def _softplus(z):
    return jnp.maximum(z, 0.0) + jnp.log1p(jnp.exp(-jnp.abs(z)))


def _log_sigmoid(z):
    return jnp.minimum(z, 0.0) - jnp.log1p(jnp.exp(-jnp.abs(z)))


def _rmsnorm(x, g):
    ms = jnp.mean(x * x, axis=-1, keepdims=True)
    return (x * lax.rsqrt(ms + RMS_EPS)) * g


def _lane_tile(x, n):
    return x if n == 1 else jnp.concatenate([x] * n, axis=1)


def _params(*sem):
    return pltpu.CompilerParams(dimension_semantics=sem, vmem_limit_bytes=VMEM_LIMIT)


def _norm_proj_rows_kernel(*refs, d, gate):
    x_ref, g_ref, w_ref = refs[:3]
    refs = refs[3:]
    if gate:
        wf_ref, bf_ref = refs[:2]
        refs = refs[2:]
    q_ref, k_ref, v_ref = refs[:3]
    hb = _rmsnorm(x_ref[...], g_ref[...]).astype(BF16)
    y = _dot(hb, w_ref[...])
    q_ref[...] = y[:, :d] * (HEAD_DIM ** -0.5)
    k_ref[...] = y[:, d:2 * d]
    v_ref[...] = y[:, 2 * d:]
    if gate:
        lg = _dot(hb, wf_ref[...]) + bf_ref[...]
        refs[3][...] = _log_sigmoid(lg)[:, :N_HEADS]


def _norm_proj_rows(x, g, w, wf=None, bf=None, *, tm):
    t, d = x.shape
    gate = wf is not None
    row = lambda i: (i, 0)
    const = lambda i: (0, 0)
    in_specs = [pl.BlockSpec((tm, d), row), pl.BlockSpec((1, d), const),
                pl.BlockSpec((d, 3 * d), const)]
    args = [x, g.reshape(1, d), w]
    out_shape = [jax.ShapeDtypeStruct((t, d), F32)] * 3
    out_specs = [pl.BlockSpec((tm, d), row)] * 3
    if gate:
        in_specs += [pl.BlockSpec((d, LANES), const), pl.BlockSpec((1, LANES), const)]
        args += [wf, bf]
        out_shape.append(jax.ShapeDtypeStruct((t, N_HEADS), F32))
        out_specs.append(pl.BlockSpec((tm, N_HEADS), row))
    return pl.pallas_call(
        functools.partial(_norm_proj_rows_kernel, d=d, gate=gate),
        grid=(t // tm,), in_specs=in_specs, out_specs=out_specs, out_shape=out_shape,
        compiler_params=_params("arbitrary"),
        name="norm_proj_rows_gate" if gate else "norm_proj_rows")(*args)


def _aug_constants(tm):
    width = N_HEADS * HEAD_PAD
    place_q = np.zeros((3, LANES, width), np.float32)
    place_k = np.zeros((3, N_HEADS * 16, N_HEADS), np.float32)
    q_ones = np.zeros((1, width), np.float32)
    k_ones = np.zeros((N_HEADS * 16, LANES), np.float32)
    for h in range(N_HEADS):
        for r in range(3):
            place_q[r, h, h * HEAD_PAD + AUG + r] = 1.0
            place_k[r, h * 16 + 3 + r, h] = -1.0
            q_ones[0, h * HEAD_PAD + AUG + 3 + r] = 1.0
            k_ones[h * 16 + r, :] = 1.0
    tri = np.tril(np.ones((tm, tm), np.float32))
    return (jnp.asarray(place_q, BF16), jnp.asarray(place_k, BF16), jnp.asarray(q_ones),
            jnp.asarray(k_ones), jnp.asarray(tri, BF16), jnp.asarray(tri.T, BF16))


def _norm_proj_t_kernel(*refs, d, gate, tm, tiles_per_seq):
    x_ref, g_ref, wq_ref, wkt_ref, wvt_ref = refs[:5]
    refs = refs[5:]
    if gate:
        (wf_ref, bf_ref, wft_ref, bft_ref, pq_ref, pk_ref, qones_ref, kones_ref,
         tri_ref, trit_ref) = refs[:10]
        refs = refs[10:]
    q_ref, kt_ref, vt_ref, ka_ref, va_ref = refs[:5]
    refs = refs[5:]

    hb = _rmsnorm(x_ref[...], g_ref[...]).astype(BF16)
    q = _dot(hb, wq_ref[...]) * (HEAD_DIM ** -0.5)
    kt = _dot_nt(wkt_ref[...], hb)
    vt = _dot_nt(wvt_ref[...], hb)
    kt_ref[...] = kt
    vt_ref[...] = vt
    ktb = kt.astype(BF16)
    vtb = vt.astype(BF16)
    zeros = jnp.zeros((HEAD_PAD - HEAD_DIM - 16, tm), BF16)
    ones_row = (lax.broadcasted_iota(jnp.int32, (16, tm), 0) == 0).astype(BF16)

    if gate:
        logft_ref, carry_ref, carryt_ref = refs[:3]

        @pl.when(pl.program_id(0) % tiles_per_seq == 0)
        def _():
            carry_ref[...] = jnp.zeros_like(carry_ref)
            carryt_ref[...] = jnp.zeros_like(carryt_ref)

        logf = _log_sigmoid(_dot(hb, wf_ref[...]) + bf_ref[...])
        cum = _dot3(tri_ref[...], _split3(logf)) + carry_ref[0:1, :]
        carry_ref[...] = jnp.broadcast_to(cum[-1:, :], carry_ref.shape)
        c1, c2, c3 = _split3(cum)
        q = q + ((_dot(c1, pq_ref[0]) + _dot(c2, pq_ref[1])) + _dot(c3, pq_ref[2])) + qones_ref[...]
        logft = _log_sigmoid(_dot_nt(wft_ref[...], hb) + bft_ref[:, 0:1])
        logft_ref[...] = logft
        cumt = _dot3_right(_split3(logft), trit_ref[...]) + carryt_ref[:, 0:1]
        carryt_ref[...] = jnp.broadcast_to(cumt[:, -1:], carryt_ref.shape)
        t1, t2, t3 = _split3(cumt)
        kaug = ((_dot(pk_ref[0], t1) + _dot(pk_ref[1], t2)) + _dot(pk_ref[2], t3)
                + _lane_tile(kones_ref[...], tm // LANES)).astype(BF16)

    q_ref[...] = q.astype(BF16)
    for h in range(N_HEADS):
        lo, base = h * HEAD_DIM, h * HEAD_PAD
        ka_ref[base:base + HEAD_DIM, :] = ktb[lo:lo + HEAD_DIM, :]
        va_ref[base:base + HEAD_DIM, :] = vtb[lo:lo + HEAD_DIM, :]
        ka_ref[base + HEAD_DIM:base + HEAD_DIM + 16, :] = (
            kaug[h * 16:(h + 1) * 16, :] if gate else jnp.zeros((16, tm), BF16))
        va_ref[base + HEAD_DIM:base + HEAD_DIM + 16, :] = ones_row
        ka_ref[base + HEAD_DIM + 16:base + HEAD_PAD, :] = zeros
        va_ref[base + HEAD_DIM + 16:base + HEAD_PAD, :] = zeros


def _norm_proj_t(x, g, wq_pad, wkt, wvt, gate_args=None, *, tm, seq_len):
    t, d = x.shape
    nb = t // seq_len
    tiles_per_seq = seq_len // tm
    width = N_HEADS * HEAD_PAD
    gate = gate_args is not None
    row = lambda i: (i, 0)
    const = lambda i: (0, 0)
    col = lambda i: (i // tiles_per_seq, 0, i % tiles_per_seq)
    in_specs = [pl.BlockSpec((tm, d), row), pl.BlockSpec((1, d), const),
                pl.BlockSpec((d, width), const), pl.BlockSpec((d, d), const),
                pl.BlockSpec((d, d), const)]
    args = [x, g.reshape(1, d), wq_pad, wkt, wvt]
    out_shape = [jax.ShapeDtypeStruct((t, width), BF16),
                 jax.ShapeDtypeStruct((nb, d, seq_len), F32),
                 jax.ShapeDtypeStruct((nb, d, seq_len), F32),
                 jax.ShapeDtypeStruct((nb, width, seq_len), BF16),
                 jax.ShapeDtypeStruct((nb, width, seq_len), BF16)]
    out_specs = [pl.BlockSpec((tm, width), row), pl.BlockSpec((None, d, tm), col),
                 pl.BlockSpec((None, d, tm), col), pl.BlockSpec((None, width, tm), col),
                 pl.BlockSpec((None, width, tm), col)]
    scratch = []
    if gate:
        wf, bf, wft, bft = gate_args
        consts = _aug_constants(tm)
        args += [wf, bf, wft, bft, *consts]
        in_specs += [pl.BlockSpec((d, LANES), const), pl.BlockSpec((1, LANES), const),
                     pl.BlockSpec((N_HEADS, d), const), pl.BlockSpec((N_HEADS, LANES), const),
                     pl.BlockSpec(consts[0].shape, lambda i: (0, 0, 0)),
                     pl.BlockSpec(consts[1].shape, lambda i: (0, 0, 0)),
                     pl.BlockSpec(consts[2].shape, const), pl.BlockSpec(consts[3].shape, const),
                     pl.BlockSpec((tm, tm), const), pl.BlockSpec((tm, tm), const)]
        out_shape.append(jax.ShapeDtypeStruct((nb, N_HEADS, seq_len), F32))
        out_specs.append(pl.BlockSpec((None, N_HEADS, tm), col))
        scratch = [pltpu.VMEM((8, LANES), F32), pltpu.VMEM((N_HEADS, LANES), F32)]
    return pl.pallas_call(
        functools.partial(_norm_proj_t_kernel, d=d, gate=gate, tm=tm, tiles_per_seq=tiles_per_seq),
        grid=(t // tm,), in_specs=in_specs, out_specs=out_specs, out_shape=out_shape,
        scratch_shapes=scratch, compiler_params=_params("arbitrary"),
        name="norm_proj_t_gate" if gate else "norm_proj_t")(*args)


def _sb_tiles(qs, kts, vts, u, ones, carries, accs, mask, token_major=False, chained=False):
    n = len(kts)
    keys = u.shape[0]
    s = [_dot_nt(qs[i], kts[i]) if token_major else _dot(qs[i], kts[i]) for i in range(n)]
    lk = [-_softplus(x) for x in s]
    if mask is not None:
        lk = [jnp.where(mask, x, 0.0) for x in lk]
    hi = [x.astype(BF16) for x in lk]
    lo = [(x - h.astype(F32)).astype(BF16) for x, h in zip(lk, hi)]
    suffix = [_dot(h, u) + _dot(l, u) for h, l in zip(hi, lo)]
    total = [_dot(h, ones) + _dot(l, ones) for h, l in zip(hi, lo)]
    if chained:
        before = [carries[0]]
        for i in range(n):
            before.append(before[-1] + total[i])
        new_carries = [before[-1]]
    else:
        before = list(carries)
        new_carries = [c + t for c, t in zip(carries, total)]
    a = [jnp.exp(s[i] + suffix[i] + _lane_tile(before[i], keys // LANES)) for i in range(n)]
    if mask is not None:
        a = [jnp.where(mask, x, 0.0) for x in a]
    pv = [_dot(a[i].astype(BF16), vts[i]) if token_major else _dot_nt(a[i].astype(BF16), vts[i])
          for i in range(n)]
    if chained:
        acc = accs[0]
        for x in pv:
            acc = acc + x
        return new_carries, [acc]
    return new_carries, [acc + x for acc, x in zip(accs, pv)]


def _sb_prompt_kernel(q_ref, k_ref, v_ref, u_ref, ones_ref, o_ref, *, tq, group):
    i = pl.program_id(2)
    u = u_ref[...]
    ones = ones_ref[...]
    row = lax.broadcasted_iota(jnp.int32, (tq, tq), 0)
    col = lax.broadcasted_iota(jnp.int32, (tq, tq), 1)
    head = lambda g: slice(g * HEAD_PAD, (g + 1) * HEAD_PAD)

    def tile(j, carries, accs, mask):
        start = pl.multiple_of(j * tq, tq)
        carries, accs = _sb_tiles(
            [q_ref[:, head(g)] for g in range(group)],
            [k_ref[head(g), pl.ds(start, tq)] for g in range(group)],
            [v_ref[head(g), pl.ds(start, tq)] for g in range(group)],
            u, ones, carries, accs, mask)
        return tuple(carries), tuple(accs)

    def live(carries):
        top = carries[0]
        for carry in carries[1:]:
            top = jnp.maximum(top, carry)
        return (jnp.max(top) > -SB_EXIT).astype(jnp.int32)

    zero = (jnp.zeros((tq, LANES), F32),) * group
    carries, accs = tile(i, zero, zero, col < row)

    def body(st):
        j, _, carries, accs = st
        carries, accs = tile(j, carries, accs, None)
        return j - 1, live(carries), carries, accs

    accs = lax.while_loop(lambda st: jnp.logical_and(st[0] >= 0, st[1] > 0), body,
                          (i - 1, live(carries), carries, accs))[3]
    for g in range(group):
        o_ref[:, head(g)] = accs[g].astype(BF16)


def _suffix_ones(n):
    return jnp.asarray(np.tril(np.ones((n, n), np.float32)), BF16)


def _attn_specs(s, tq, group):
    tile = pl.BlockSpec((None, tq, group * HEAD_PAD), lambda bi, h, i: (bi, i, h))
    resident = pl.BlockSpec((None, group * HEAD_PAD, s), lambda bi, h, i: (bi, h, 0))
    return tile, resident


def _sb_prompt(q, kat, vat, *, tq, group):
    b, s, width = q.shape
    tile, resident = _attn_specs(s, tq, group)
    const = lambda bi, h, i: (0, 0)
    return pl.pallas_call(
        functools.partial(_sb_prompt_kernel, tq=tq, group=group),
        grid=(b, N_HEADS // group, s // tq),
        in_specs=[tile, resident, resident, pl.BlockSpec((tq, tq), const),
                  pl.BlockSpec((tq, LANES), const)],
        out_specs=tile, out_shape=jax.ShapeDtypeStruct((b, s, width), BF16),
        compiler_params=_params("arbitrary", "arbitrary", "arbitrary"),
        name="sb_prompt")(q, kat, vat, _suffix_ones(tq), jnp.ones((tq, LANES), BF16))


def _fox_prompt_kernel(q_ref, k_ref, v_ref, o_ref, m_ref, acc_ref, *, tq, group):
    i = pl.program_id(2)
    row = lax.broadcasted_iota(jnp.int32, (tq, tq), 0)
    col = lax.broadcasted_iota(jnp.int32, (tq, tq), 1)
    m_ref[...] = jnp.full(m_ref.shape, NEG_BIG, F32)
    acc_ref[...] = jnp.zeros_like(acc_ref)
    head = lambda g: slice(g * HEAD_PAD, (g + 1) * HEAD_PAD)

    def tile(j, mask):
        start = pl.multiple_of(j * tq, tq)
        heads = range(group)
        s = [_dot(q_ref[:, head(g)], k_ref[head(g), pl.ds(start, tq)]) for g in heads]
        if mask is not None:
            s = [jnp.where(mask, x, NEG_BIG) for x in s]
        m_prev = [m_ref[g] for g in heads]
        m_next = [jnp.maximum(m_prev[g], jnp.max(s[g], axis=1, keepdims=True)) for g in heads]
        p = [jnp.exp(s[g] - _lane_tile(m_next[g], tq // LANES)).astype(BF16) for g in heads]
        pv = [_dot_nt(p[g], v_ref[head(g), pl.ds(start, tq)]) for g in heads]
        for g in heads:
            acc_ref[g] = acc_ref[g] * jnp.exp(m_prev[g] - m_next[g]) + pv[g]
            m_ref[g] = m_next[g]

    tile(i, col <= row)

    @pl.loop(0, i)
    def _(j):
        tile(j, None)

    for g in range(group):
        acc = acc_ref[g]
        lane = lax.broadcasted_iota(jnp.int32, acc.shape, 1)
        denom = jnp.sum(jnp.where(lane == AUG, acc, 0.0), axis=1, keepdims=True)
        o_ref[:, head(g)] = (acc / denom).astype(BF16)


def _fox_prompt(q, kat, vat, *, tq, group):
    b, s, width = q.shape
    tile, resident = _attn_specs(s, tq, group)
    return pl.pallas_call(
        functools.partial(_fox_prompt_kernel, tq=tq, group=group),
        grid=(b, N_HEADS // group, s // tq),
        in_specs=[tile, resident, resident],
        out_specs=tile, out_shape=jax.ShapeDtypeStruct((b, s, width), BF16),
        scratch_shapes=[pltpu.VMEM((group, tq, LANES), F32), pltpu.VMEM((group, tq, HEAD_PAD), F32)],
        compiler_params=_params("arbitrary", "arbitrary", "arbitrary"),
        name="fox_prompt")(q, kat, vat)


def _head_lane_mask(d):
    r = lax.broadcasted_iota(jnp.int32, (N_HEADS, d), 0)
    lane = lax.broadcasted_iota(jnp.int32, (N_HEADS, d), 1)
    return lane // HEAD_DIM == r


def _block_diag_query(q):
    tokens, d = q.shape
    mask = _head_lane_mask(d)
    rows = [jnp.where(mask, jnp.broadcast_to(q[t:t + 1, :], (N_HEADS, d)), 0.0)
            for t in range(tokens)]
    return jnp.concatenate(rows, axis=0).astype(BF16)


def _collect_heads(acc, tokens):
    mask = _head_lane_mask(acc.shape[1])
    rows = [jnp.sum(jnp.where(mask, acc[t * N_HEADS:(t + 1) * N_HEADS, :], 0.0), axis=0, keepdims=True)
            for t in range(tokens)]
    return jnp.concatenate(rows, axis=0)


def _pad_rows(x, rows):
    return jnp.concatenate([x, jnp.zeros((rows - x.shape[0], x.shape[1]), x.dtype)], axis=0)


def _new_token_mask(rows, page, strict):
    r = lax.broadcasted_iota(jnp.int32, (rows, page), 0) // N_HEADS
    c = lax.broadcasted_iota(jnp.int32, (rows, page), 1)
    return c < r if strict else c <= r


def _sb_decode_kernel(*refs, tokens, page, npp, first):
    q_ref = refs[2 - first]
    refs = refs[3 - first:]
    if first:
        kn_ref, vn_ref = refs[:2]
    else:
        acc_in_ref, carry_in_ref = refs[:2]
    refs = refs[2:]
    kc_refs, vc_refs = refs[:npp], refs[npp:2 * npp]
    u_ref, ones_ref = refs[2 * npp:2 * npp + 2]
    refs = refs[2 * npp + 2:]
    if first:
        acc_ref, carry_ref, qbd_ref, live_ref = refs
    else:
        o_ref, acc_ref, carry_ref, qbd_ref, live_ref = refs
    step = pl.program_id(1)
    rows = tokens * N_HEADS
    u = u_ref[...]
    ones = ones_ref[...]

    def is_live(carry):
        return (jnp.max(carry) > -SB_EXIT).astype(jnp.int32)

    @pl.when(step == 0)
    def _():
        qbd = _block_diag_query(q_ref[...])
        qbd_ref[...] = qbd
        if first:
            (carry,), (acc,) = _sb_tiles(
                [qbd], [_pad_rows(kn_ref[...], page).astype(BF16)],
                [_pad_rows(vn_ref[...], page).astype(BF16)], u, ones,
                [jnp.zeros((rows, LANES), F32)], [jnp.zeros(acc_ref.shape, F32)],
                _new_token_mask(rows, page, True), token_major=True)
        else:
            carry, acc = carry_in_ref[...], acc_in_ref[...]
        acc_ref[...] = acc
        carry_ref[...] = carry
        live_ref[0] = is_live(carry)

    @pl.when(live_ref[0] > 0)
    def _():
        (carry,), (acc,) = _sb_tiles(
            [qbd_ref[...]] * npp, [kc[...].astype(BF16) for kc in kc_refs],
            [vc[...].astype(BF16) for vc in vc_refs], u, ones,
            [carry_ref[...]], [acc_ref[...]], None, chained=True)
        acc_ref[...] = acc
        carry_ref[...] = carry
        live_ref[0] = is_live(carry)

    if not first:
        @pl.when(step == pl.num_programs(1) - 1)
        def _():
            o_ref[...] = _collect_heads(acc_ref[...], tokens)


def _collect_kernel(acc_ref, o_ref, *, tokens):
    o_ref[...] = _collect_heads(acc_ref[...], tokens)


def _sb_decode(q, kn, vn, cache_kt, cache_vt, page_table, *, head_pages, npp):
    b, tokens, d = q.shape
    n_pages = page_table.shape[1]
    page = cache_kt.shape[2]
    rows = tokens * N_HEADS
    u, ones = _suffix_ones(page), jnp.ones((page, LANES), BF16)
    new = pl.BlockSpec((None, tokens, d), lambda bi, s, *_: (bi, 0, 0))
    consts = [pl.BlockSpec((page, page), lambda bi, s, *_: (0, 0)),
              pl.BlockSpec((page, LANES), lambda bi, s, *_: (0, 0))]
    acc_spec = pl.BlockSpec((None, rows, d), lambda bi, s, *_: (bi, 0, 0))
    carry_spec = pl.BlockSpec((None, rows, LANES), lambda bi, s, *_: (bi, 0, 0))
    scratch = [pltpu.VMEM((rows, d), BF16), pltpu.SMEM((1,), jnp.int32)]

    def paged(first_page, gated):
        def make(i):
            def index(bi, s, pt, *live):
                p = first_page - (s * npp + i)
                if gated:
                    p = jnp.where(live[0][bi] > 0, p, first_page)
                return (pt[bi, p], 0, 0)
            return pl.BlockSpec((None, d, page), index)
        return [make(i) for i in range(npp)] * 2

    acc, carry = pl.pallas_call(
        functools.partial(_sb_decode_kernel, tokens=tokens, page=page, npp=npp, first=True),
        grid_spec=pltpu.PrefetchScalarGridSpec(
            num_scalar_prefetch=1, grid=(b, head_pages // npp),
            in_specs=[new, new, new] + paged(n_pages - 1, False) + consts,
            out_specs=[acc_spec, carry_spec], scratch_shapes=scratch),
        out_shape=[jax.ShapeDtypeStruct((b, rows, d), F32), jax.ShapeDtypeStruct((b, rows, LANES), F32)],
        compiler_params=_params("arbitrary", "arbitrary"),
        name="sb_decode_head")(page_table, q, kn, vn, *([cache_kt] * npp), *([cache_vt] * npp), u, ones)
    live = (jnp.max(carry, axis=(1, 2)) > -SB_EXIT).astype(jnp.int32)
    out_shape = jax.ShapeDtypeStruct((b, tokens, d), F32)

    def older_pages():
        return pl.pallas_call(
            functools.partial(_sb_decode_kernel, tokens=tokens, page=page, npp=npp, first=False),
            grid_spec=pltpu.PrefetchScalarGridSpec(
                num_scalar_prefetch=2, grid=(b, (n_pages - head_pages) // npp),
                in_specs=[new, acc_spec, carry_spec] + paged(n_pages - 1 - head_pages, True) + consts,
                out_specs=new,
                scratch_shapes=[pltpu.VMEM((rows, d), F32), pltpu.VMEM((rows, LANES), F32)] + scratch),
            out_shape=out_shape, compiler_params=_params("arbitrary", "arbitrary"),
            name="sb_decode_tail")(page_table, live, q, acc, carry, *([cache_kt] * npp),
                                   *([cache_vt] * npp), u, ones)

    def finished():
        return pl.pallas_call(
            functools.partial(_collect_kernel, tokens=tokens), grid=(b,), in_specs=[pl.BlockSpec((None, rows, d), lambda bi: (bi, 0, 0))],
            out_specs=pl.BlockSpec((None, tokens, d), lambda bi: (bi, 0, 0)),
            out_shape=out_shape, compiler_params=_params("arbitrary"),
            name="sb_decode_done")(acc)

    return lax.cond(jnp.any(live > 0), older_pages, finished)


def _fox_decode_kernel(*refs, tokens, page, npp):
    q_ref, kn_ref, vn_ref, lfn_ref = refs[1:5]
    refs = refs[5:]
    kc_refs, vc_refs, lc_refs = refs[:npp], refs[npp:2 * npp], refs[2 * npp:3 * npp]
    us_ref, ones_ref, o_ref, qbd_ref, acc_ref, m_ref, l_ref, fcarry_ref = refs[3 * npp:]
    step = pl.program_id(1)
    rows = tokens * N_HEADS
    us = us_ref[...]
    ones = ones_ref[...]

    def biases(lgs):
        pieces = [_split3(lg) for lg in lgs]
        suffix = [_dot3_right(p, us) for p in pieces]
        total = [_dot3_right(p, ones) for p in pieces]
        carry = fcarry_ref[...]
        out = []
        for sfx, tot in zip(suffix, total):
            out.append(jnp.concatenate([sfx + carry] * tokens, axis=0))
            carry = carry + tot
        fcarry_ref[...] = carry
        return out

    def update(s_list, pv):
        m_prev = m_ref[...]
        m_next = m_prev
        for s in s_list:
            m_next = jnp.maximum(m_next, jnp.max(s, axis=1, keepdims=True))
        alpha = jnp.exp(m_prev - m_next)
        p = [jnp.exp(s - m_next).astype(BF16) for s in s_list]
        outs = [v(x) for v, x in zip(pv, p)]
        sums = [_dot(x, ones) for x in p]
        acc = acc_ref[...] * _lane_tile(alpha, acc_ref.shape[1] // LANES)
        l = l_ref[...] * alpha
        for o, r in zip(outs, sums):
            acc = acc + o
            l = l + r
        acc_ref[...] = acc
        l_ref[...] = l
        m_ref[...] = m_next

    @pl.when(step == 0)
    def _():
        qbd = _block_diag_query(q_ref[...])
        qbd_ref[...] = qbd
        m_ref[...] = jnp.full(m_ref.shape, NEG_BIG, F32)
        l_ref[...] = jnp.zeros_like(l_ref)
        acc_ref[...] = jnp.zeros_like(acc_ref)
        fcarry_ref[...] = jnp.zeros_like(fcarry_ref)
        lgn = _pad_rows(jnp.concatenate(
            [lfn_ref[...], jnp.zeros((tokens, LANES - N_HEADS), F32)], axis=1), page).T[:N_HEADS, :]
        kn = _pad_rows(kn_ref[...], page).astype(BF16)
        vn = _pad_rows(vn_ref[...], page).astype(BF16)
        s = _dot_nt(qbd, kn) + biases([lgn])[0]
        s = jnp.where(_new_token_mask(rows, page, False), s, NEG_BIG)
        update([s], [lambda p: _dot(p, vn)])

    qbd = qbd_ref[...]
    qk = [_dot(qbd, kc[...].astype(BF16)) for kc in kc_refs]
    bias = biases([lc[...] for lc in lc_refs])
    update([a + b for a, b in zip(qk, bias)],
           [lambda p, vc=vc: _dot_nt(p, vc[...].astype(BF16)) for vc in vc_refs])

    @pl.when(step == pl.num_programs(1) - 1)
    def _():
        out = acc_ref[...] / _lane_tile(l_ref[...], acc_ref.shape[1] // LANES)
        o_ref[...] = _collect_heads(out, tokens)


def _fox_decode(q, kn, vn, lfn, cache_kt, cache_vt, cache_lt, page_table, *, npp):
    b, tokens, d = q.shape
    n_pages = page_table.shape[1]
    page = cache_kt.shape[2]
    rows = tokens * N_HEADS
    new = pl.BlockSpec((None, tokens, d), lambda bi, s, pt: (bi, 0, 0))

    def paged(width):
        def make(i):
            return pl.BlockSpec((None, width, page),
                                lambda bi, s, pt: (pt[bi, n_pages - 1 - (s * npp + i)], 0, 0))
        return [make(i) for i in range(npp)]

    strict = jnp.asarray(np.triu(np.ones((page, page), np.float32), k=1).T, BF16)
    return pl.pallas_call(
        functools.partial(_fox_decode_kernel, tokens=tokens, page=page, npp=npp),
        grid_spec=pltpu.PrefetchScalarGridSpec(
            num_scalar_prefetch=1, grid=(b, n_pages // npp),
            in_specs=[new, new, new, pl.BlockSpec((None, tokens, N_HEADS), lambda bi, s, pt: (bi, 0, 0))]
            + paged(d) + paged(d) + paged(N_HEADS)
            + [pl.BlockSpec((page, page), lambda bi, s, pt: (0, 0)),
               pl.BlockSpec((page, LANES), lambda bi, s, pt: (0, 0))],
            out_specs=new,
            scratch_shapes=[pltpu.VMEM((rows, d), BF16), pltpu.VMEM((rows, d), F32),
                            pltpu.VMEM((rows, LANES), F32), pltpu.VMEM((rows, LANES), F32),
                            pltpu.VMEM((N_HEADS, LANES), F32)]),
        out_shape=jax.ShapeDtypeStruct((b, tokens, d), F32),
        compiler_params=_params("arbitrary", "arbitrary"),
        name="fox_decode")(page_table, q, kn, vn, lfn, *([cache_kt] * npp), *([cache_vt] * npp),
                           *([cache_lt] * npp), strict, jnp.ones((page, LANES), BF16))


def _top2_gates(logits, n_experts):
    lane = lax.broadcasted_iota(jnp.int32, logits.shape, 1).astype(F32)
    lg = jnp.where(lane < n_experts, logits, NEG_BIG)
    m1 = jnp.max(lg, axis=1, keepdims=True)
    i1 = jnp.min(jnp.where(lg == m1, lane, float(LANES)), axis=1, keepdims=True)
    lg2 = jnp.where(lane == i1, NEG_BIG, lg)
    m2 = jnp.max(lg2, axis=1, keepdims=True)
    i2 = jnp.min(jnp.where(lg2 == m2, lane, float(LANES)), axis=1, keepdims=True)
    e2 = jnp.exp(m2 - m1)
    w1 = 1.0 / (1.0 + e2)
    gate = jnp.where(lane == i1, w1, 0.0) + jnp.where(lane == i2, e2 * w1, 0.0)
    return gate, jnp.logical_or(lane == i1, lane == i2)


def _router_logits(h, router):
    logits = None
    r1, r2, r3 = _split3(router)
    h1, h2, h3 = _split3(h)
    for a, b in ((h3, r1), (h2, r2), (h1, r3), (h2, r1), (h1, r2), (h1, r1)):
        t = _dot(a, b)
        logits = t if logits is None else logits + t
    return logits


def _swiglu_partial(hb, wg_ref, wu_ref, wd_ref):
    g = _dot(hb, wg_ref[...])
    u = _dot(hb, wu_ref[...])
    return _dot(((g * jax.nn.sigmoid(g)) * u).astype(BF16), wd_ref[...])


def _ffn_kernel(x_ref, o_ref, wo_ref, g_ref, wg_ref, wu_ref, wd_ref, out_ref, x1_ref, hb_ref, acc_ref):
    f = pl.program_id(1)

    @pl.when(f == 0)
    def _():
        x1 = x_ref[...] + _dot(o_ref[...].astype(BF16), wo_ref[...])
        x1_ref[...] = x1
        hb_ref[...] = _rmsnorm(x1, g_ref[...]).astype(BF16)
        acc_ref[...] = jnp.zeros_like(acc_ref)

    acc_ref[...] += _swiglu_partial(hb_ref[...], wg_ref, wu_ref, wd_ref)

    @pl.when(f == pl.num_programs(1) - 1)
    def _():
        out_ref[...] = x1_ref[...] + acc_ref[...]


def _ffn(x, o, wo, g, wg, wu, wd, *, tm, tf):
    t, d = x.shape
    dff = wg.shape[1]
    row = lambda i, f: (i, 0)
    const = lambda i, f: (0, 0)
    return pl.pallas_call(
        _ffn_kernel, grid=(t // tm, dff // tf),
        in_specs=[pl.BlockSpec((tm, d), row), pl.BlockSpec((tm, o.shape[1]), row),
                  pl.BlockSpec(wo.shape, const), pl.BlockSpec((1, d), const),
                  pl.BlockSpec((d, tf), lambda i, f: (0, f)), pl.BlockSpec((d, tf), lambda i, f: (0, f)),
                  pl.BlockSpec((tf, d), lambda i, f: (f, 0))],
        out_specs=pl.BlockSpec((tm, d), row), out_shape=jax.ShapeDtypeStruct((t, d), F32),
        scratch_shapes=[pltpu.VMEM((tm, d), F32), pltpu.VMEM((tm, d), BF16), pltpu.VMEM((tm, d), F32)],
        compiler_params=_params("arbitrary", "arbitrary"),
        name="dense_ffn")(x, o, wo, g.reshape(1, d), wg, wu, wd)


SEG_ALIGN = 16
MOE_CHUNK = 256
MOE_ROWS = 512


def _moe_pre_kernel(x_ref, o_ref, wo_ref, g_ref, router_ref, lower_ref, upper_ref,
                    x1_ref, hb_ref, gate_ref, rank_ref, rankt_ref, count_ref, *, n_experts):
    x1 = x_ref[...] + _dot(o_ref[...].astype(BF16), wo_ref[...])
    x1_ref[...] = x1
    h = _rmsnorm(x1, g_ref[...])
    hb_ref[...] = h.astype(BF16)
    gate, routed = _top2_gates(_router_logits(h, router_ref[...]), n_experts)
    gate_ref[...] = gate
    mask = routed.astype(F32)
    rank = _dot(lower_ref[...], mask.astype(BF16))
    rank_ref[...] = jnp.where(routed, rank, -1.0)
    mask_t = mask.T
    rank_t = _dot(mask_t.astype(BF16), upper_ref[...])
    rankt_ref[...] = jnp.where(mask_t > 0.0, rank_t, -1.0)[:8, :]
    count_ref[...] = jnp.broadcast_to(jnp.sum(mask, axis=0, keepdims=True), count_ref.shape)


def _moe_dispatch_kernel(seg_ref, hb_ref, rankt_ref, init_ref, xs_ref):
    del seg_ref, init_ref
    e = pl.program_id(1)
    n = hb_ref.shape[0]
    slot = lax.broadcasted_iota(jnp.int32, (n, n), 0).astype(F32)
    pick = (slot == rankt_ref[pl.ds(e, 1), :]).astype(BF16)
    xs_ref[...] = _dot(pick, hb_ref[...]).astype(BF16)


def _moe_expert_kernel(expert_ref, valid_ref, xs_ref, wg_ref, wu_ref, wd_ref, ys_ref, acc_ref):
    del expert_ref
    r, f = pl.program_id(0), pl.program_id(1)

    @pl.when(valid_ref[r] > 0)
    def _():
        @pl.when(f == 0)
        def _():
            acc_ref[...] = jnp.zeros_like(acc_ref)

        acc_ref[...] += _swiglu_partial(xs_ref[...], wg_ref, wu_ref, wd_ref)

        @pl.when(f == pl.num_programs(1) - 1)
        def _():
            ys_ref[...] = acc_ref[...]

    @pl.when(valid_ref[r] == 0)
    def _():
        ys_ref[...] = jnp.zeros_like(ys_ref)


def _moe_combine_kernel(seg_ref, *refs, n_experts):
    del seg_ref
    ys_refs = refs[:n_experts]
    rank_ref, gate_ref, x1_ref, gf_ref, out_ref = refs[n_experts:]
    n = rank_ref.shape[0]
    rank, gate = rank_ref[...], gate_ref[...]
    lane = lax.broadcasted_iota(jnp.int32, rank.shape, 1)
    slot = lax.broadcasted_iota(jnp.int32, (n, n), 1).astype(F32)
    y = jnp.zeros(x1_ref.shape, F32)
    for e, ys_ref in enumerate(ys_refs):
        column = lambda a: jnp.sum(jnp.where(lane == e, a, 0.0), axis=1, keepdims=True)
        pick = (slot == column(rank)).astype(BF16)
        rows = ys_ref[...]
        hi = rows.astype(BF16)
        lo = (rows - hi.astype(F32)).astype(BF16)
        y = y + column(gate) * (_dot(pick, hi) + _dot(pick, lo))
    out_ref[...] = _rmsnorm(x1_ref[...] + y, gf_ref[...])


def _moe(x, o, wo, g, router, wg, wu, wd, g_final, *, tf):
    t, d = x.shape
    n_experts, _, dff = wg.shape
    chunk = _tile(t, MOE_CHUNK)
    n_chunks = t // chunk
    row = lambda i: (i, 0)
    const = lambda i: (0, 0)
    lower = jnp.asarray(np.tril(np.ones((chunk, chunk), np.float32), k=-1), BF16)
    x1, hb, gate, rank, rank_t, counts = pl.pallas_call(
        functools.partial(_moe_pre_kernel, n_experts=n_experts),
        grid=(n_chunks,),
        in_specs=[pl.BlockSpec((chunk, d), row), pl.BlockSpec((chunk, o.shape[1]), row),
                  pl.BlockSpec(wo.shape, const), pl.BlockSpec((1, d), const),
                  pl.BlockSpec((d, LANES), const), pl.BlockSpec((chunk, chunk), const),
                  pl.BlockSpec((chunk, chunk), const)],
        out_specs=[pl.BlockSpec((chunk, d), row), pl.BlockSpec((chunk, d), row),
                   pl.BlockSpec((chunk, LANES), row), pl.BlockSpec((chunk, LANES), row),
                   pl.BlockSpec((None, 8, chunk), lambda i: (i, 0, 0)),
                   pl.BlockSpec((None, 8, LANES), lambda i: (i, 0, 0))],
        out_shape=[jax.ShapeDtypeStruct((t, d), F32), jax.ShapeDtypeStruct((t, d), BF16),
                   jax.ShapeDtypeStruct((t, LANES), F32), jax.ShapeDtypeStruct((t, LANES), F32),
                   jax.ShapeDtypeStruct((n_chunks, 8, chunk), F32),
                   jax.ShapeDtypeStruct((n_chunks, 8, LANES), F32)],
        compiler_params=_params("arbitrary"),
        name="moe_pre")(x, o, wo, g.reshape(1, d), router, lower, lower.T)

    rows_tile = MOE_ROWS
    n = counts[:, 0, :n_experts].astype(jnp.int32)
    n_pad = (n + SEG_ALIGN - 1) // SEG_ALIGN * SEG_ALIGN
    cap = (jnp.sum(n_pad, axis=0) + chunk + rows_tile - 1) // rows_tile * rows_tile
    region_end = jnp.cumsum(cap)
    region_start = region_end - cap
    seg = (region_start[None, :] + jnp.cumsum(n_pad, axis=0) - n_pad).reshape(-1)
    seg = seg // SEG_ALIGN
    total_rows = (2 * t + n_chunks * n_experts * (SEG_ALIGN - 1) + n_experts * (chunk + rows_tile - 1))
    n_tiles = -(-total_rows // rows_tile)
    tile_start = jnp.arange(n_tiles, dtype=jnp.int32) * rows_tile
    tile_valid = (tile_start < region_end[-1]).astype(jnp.int32)
    tile_expert = jnp.minimum(jnp.searchsorted(region_end, tile_start, side="right"),
                              n_experts - 1).astype(jnp.int32)

    seg_index = lambda c, e, seg_ref: (seg_ref[c * n_experts + e] * SEG_ALIGN, 0)
    xs = pl.pallas_call(
        _moe_dispatch_kernel,
        grid_spec=pltpu.PrefetchScalarGridSpec(
            num_scalar_prefetch=1, grid=(n_chunks, n_experts),
            in_specs=[pl.BlockSpec((chunk, d), lambda c, e, s: (c, 0)),
                      pl.BlockSpec((None, 8, chunk), lambda c, e, s: (c, 0, 0)),
                      pl.BlockSpec(memory_space=pl.ANY)],
            out_specs=pl.BlockSpec((pl.Element(chunk), pl.Element(d)), seg_index)),
        out_shape=jax.ShapeDtypeStruct((n_tiles * rows_tile, d), BF16),
        input_output_aliases={3: 0},
        compiler_params=_params("arbitrary", "arbitrary"),
        name="moe_dispatch")(seg, hb, rank_t, jnp.zeros((n_tiles * rows_tile, d), BF16))

    n_f = dff // tf
    f_index = lambda r, f, valid: jnp.where(valid[r] > 0, f, n_f - 1)
    ys = pl.pallas_call(
        _moe_expert_kernel,
        grid_spec=pltpu.PrefetchScalarGridSpec(
            num_scalar_prefetch=2, grid=(n_tiles, n_f),
            in_specs=[pl.BlockSpec((rows_tile, d), lambda r, f, ex, va: (r, 0)),
                      pl.BlockSpec((None, d, tf), lambda r, f, ex, va: (ex[r], 0, f_index(r, f, va))),
                      pl.BlockSpec((None, d, tf), lambda r, f, ex, va: (ex[r], 0, f_index(r, f, va))),
                      pl.BlockSpec((None, tf, d), lambda r, f, ex, va: (ex[r], f_index(r, f, va), 0))],
            out_specs=pl.BlockSpec((rows_tile, d), lambda r, f, ex, va: (r, 0)),
            scratch_shapes=[pltpu.VMEM((rows_tile, d), F32)]),
        out_shape=jax.ShapeDtypeStruct((n_tiles * rows_tile, d), F32),
        compiler_params=_params("arbitrary", "arbitrary"),
        name="moe_experts")(tile_expert, tile_valid, xs, wg, wu, wd)

    def ys_spec(e):
        return pl.BlockSpec((pl.Element(chunk), pl.Element(d)), lambda c, s: (s[c * n_experts + e] * SEG_ALIGN, 0))

    return pl.pallas_call(
        functools.partial(_moe_combine_kernel, n_experts=n_experts),
        grid_spec=pltpu.PrefetchScalarGridSpec(
            num_scalar_prefetch=1, grid=(n_chunks,),
            in_specs=[ys_spec(e) for e in range(n_experts)]
            + [pl.BlockSpec((chunk, LANES), lambda c, s: (c, 0)),
               pl.BlockSpec((chunk, LANES), lambda c, s: (c, 0)),
               pl.BlockSpec((chunk, d), lambda c, s: (c, 0)),
               pl.BlockSpec((1, d), lambda c, s: (0, 0))],
            out_specs=pl.BlockSpec((chunk, d), lambda c, s: (c, 0))),
        out_shape=jax.ShapeDtypeStruct((t, d), F32),
        compiler_params=_params("arbitrary"),
        name="moe_combine")(seg, *([ys] * n_experts), rank, gate, x1, g_final.reshape(1, d))


def _tile(n, want):
    while n % want:
        want //= 2
    return want


def _lane_tile_size(n, want):
    return next(c for c in range(want, 0, -LANES) if n % c == 0)


def _pad_heads(w, axis):
    shape = w.shape[:axis] + (N_HEADS, HEAD_DIM) + w.shape[axis + 1:]
    pad = [(0, 0)] * (len(shape))
    pad[axis + 1] = (0, HEAD_PAD - HEAD_DIM)
    padded = jnp.pad(w.reshape(shape), pad)
    return padded.reshape(w.shape[:axis] + (N_HEADS * HEAD_PAD,) + w.shape[axis + 1:])


def kernel(x_prompt, x_sample, cache_sb_k, cache_sb_v, cache_fox_k, cache_fox_v, cache_fox_logf,
           page_table, ln_attn0, w_in0, w_o0, ln_ffn0, w_gate0, w_up0, w_down0, ln_attn1, w_in1,
           b_f1, w_o1, ln_ffn1, router1, w_gate1, w_up1, w_down1, ln_final):
    b, s, d = x_prompt.shape
    db, ds, _ = x_sample.shape
    n_pool, page = cache_sb_k.shape[:2]
    n_pages = page_table.shape[1]
    bf = lambda w: w.astype(BF16)

    def attn_weights(w_in, w_o):
        w = bf(w_in[:, :3 * d])
        return (w, _pad_heads(w[:, :d], 1), w[:, d:2 * d].T, w[:, 2 * d:].T,
                bf(w_o), _pad_heads(bf(w_o), 0))

    w0_rows, wq0, wkt0, wvt0, wo0, wo0_pad = attn_weights(w_in0, w_o0)
    w1_rows, wq1, wkt1, wvt1, wo1, wo1_pad = attn_weights(w_in1, w_o1)
    wf_cols = bf(w_in1[:, 3 * d:])
    wf = jnp.pad(wf_cols, ((0, 0), (0, LANES - N_HEADS)))
    bf_row = jnp.pad(b_f1, (0, LANES - N_HEADS)).reshape(1, LANES)
    gate_args = (wf, bf_row, wf_cols.T, jnp.broadcast_to(b_f1[:, None], (N_HEADS, LANES)))
    ffn0 = (bf(w_gate0), bf(w_up0), bf(w_down0))
    ffn1 = (bf(w_gate1), bf(w_up1), bf(w_down1))
    router = jnp.pad(router1, ((0, 0), (0, LANES - router1.shape[1])))
    paged_t = lambda c: jnp.transpose(c, (0, 2, 3, 1)).reshape(n_pool, d, page)
    logf_t = jnp.transpose(cache_fox_logf, (0, 2, 1))

    tp, ts = b * s, db * ds
    tm_p, tm_s = _tile(s, 256), _tile(ts, 256)
    tq_sb, tq_fox = _tile(s, 256), _tile(s, 512)
    tf = _lane_tile_size(w_gate0.shape[1], 896)
    tmf_p, tmf_s = _tile(tp, 512), _tile(ts, 256)
    npp = next(c for c in (8, 4, 2, 1) if n_pages % c == 0 and n_pages >= 2 * c)
    xp, xs = x_prompt.reshape(tp, d), x_sample.reshape(ts, d)
    seq = lambda a: a.reshape(db, ds, -1)
    width = N_HEADS * HEAD_PAD
    heads_out = lambda t: jnp.transpose(t.reshape(b, N_HEADS, HEAD_DIM, s), (0, 3, 1, 2))

    qp, k0t, v0t, kat, vat = _norm_proj_t(xp, ln_attn0, wq0, wkt0, wvt0, tm=tm_p, seq_len=s)
    op = _sb_prompt(qp.reshape(b, s, width), kat, vat, tq=tq_sb, group=4)
    xp = _ffn(xp, op.reshape(tp, width), wo0_pad, ln_ffn0, *ffn0, tm=tmf_p, tf=tf)

    qs, k0s, v0s = _norm_proj_rows(xs, ln_attn0, w0_rows, tm=tm_s)
    os_ = _sb_decode(seq(qs), seq(k0s), seq(v0s), paged_t(cache_sb_k), paged_t(cache_sb_v),
                     page_table, head_pages=npp, npp=npp)
    xs = _ffn(xs, os_.reshape(ts, d), wo0, ln_ffn0, *ffn0, tm=tmf_s, tf=tf)

    qp, k1t, v1t, kat, vat, lft = _norm_proj_t(xp, ln_attn1, wq1, wkt1, wvt1, gate_args,
                                               tm=tm_p, seq_len=s)
    op = _fox_prompt(qp.reshape(b, s, width), kat, vat, tq=tq_fox, group=4)
    yp = _moe(xp, op.reshape(tp, width), wo1_pad, ln_ffn1, router, *ffn1, ln_final, tf=tf)

    qs, k1s, v1s, lfs = _norm_proj_rows(xs, ln_attn1, w1_rows, wf, bf_row, tm=tm_s)
    os_ = _fox_decode(seq(qs), seq(k1s), seq(v1s), seq(lfs), paged_t(cache_fox_k),
                      paged_t(cache_fox_v), logf_t, page_table, npp=npp)
    ys = _moe(xs, os_.reshape(ts, d), wo1, ln_ffn1, router, *ffn1, ln_final, tf=tf)

    hs = lambda a: a.reshape(db, ds, N_HEADS, HEAD_DIM)
    return (yp.reshape(b, s, d), ys.reshape(db, ds, d), heads_out(k0t), heads_out(v0t),
            heads_out(k1t), heads_out(v1t), jnp.transpose(lft, (0, 2, 1)),
            hs(k0s), hs(v0s), hs(k1s), hs(v1s), seq(lfs))
```

```python
import functools

import numpy as np
import jax
import jax.numpy as jnp
from jax import lax
from jax.experimental import pallas as pl
from jax.experimental.pallas import tpu as pltpu

N_HEADS = 16
HEAD_DIM = 64
LANES = 128
HEAD_PAD = 128
AUG = HEAD_DIM
RMS_EPS = 1e-6
NEG_BIG = -1e30
SB_EXIT = 90.0
VMEM_LIMIT = 56 * 1024 * 1024

F32 = jnp.float32
BF16 = jnp.bfloat16

_NT = (((1,), (1,)), ((), ()))


def _dot(a, b):
    return jnp.dot(a, b, preferred_element_type=F32)


def _dot_nt(a, b):
    return lax.dot_general(a, b, _NT, preferred_element_type=F32)


def _split3(x):
    p1 = x.astype(BF16)
    r1 = x - p1.astype(F32)
    p2 = r1.astype(BF16)
    r2 = r1 - p2.astype(F32)
    return p1, p2, r2.astype(BF16)


def _dot3(a, pieces):
    return (_dot(a, pieces[0]) + _dot(a, pieces[1])) + _dot(a, pieces[2])


def _dot3_right(pieces, b):
    return (_dot(pieces[0], b) + _dot(pieces[1], b)) + _dot(pieces[2], b)


def _softplus(z):
    return jnp.maximum(z, 0.0) + jnp.log1p(jnp.exp(-jnp.abs(z)))


def _log_sigmoid(z):
    return jnp.minimum(z, 0.0) - jnp.log1p(jnp.exp(-jnp.abs(z)))


def _rmsnorm(x, g):
    ms = jnp.mean(x * x, axis=-1, keepdims=True)
    return (x * lax.rsqrt(ms + RMS_EPS)) * g


def _lane_tile(x, n):
    return x if n == 1 else jnp.concatenate([x] * n, axis=1)


def _params(*sem):
    return pltpu.CompilerParams(dimension_semantics=sem, vmem_limit_bytes=VMEM_LIMIT)


def _norm_proj_rows_kernel(*refs, d, gate):
    x_ref, g_ref, w_ref = refs[:3]
    refs = refs[3:]
    if gate:
        wf_ref, bf_ref = refs[:2]
        refs = refs[2:]
    q_ref, k_ref, v_ref = refs[:3]
    hb = _rmsnorm(x_ref[...], g_ref[...]).astype(BF16)
    y = _dot(hb, w_ref[...])
    q_ref[...] = y[:, :d] * (HEAD_DIM ** -0.5)
    k_ref[...] = y[:, d:2 * d]
    v_ref[...] = y[:, 2 * d:]
    if gate:
        lg = _dot(hb, wf_ref[...]) + bf_ref[...]
        refs[3][...] = _log_sigmoid(lg)[:, :N_HEADS]


def _norm_proj_rows(x, g, w, wf=None, bf=None, *, tm):
    t, d = x.shape
    gate = wf is not None
    row = lambda i: (i, 0)
    const = lambda i: (0, 0)
    in_specs = [pl.BlockSpec((tm, d), row), pl.BlockSpec((1, d), const),
                pl.BlockSpec((d, 3 * d), const)]
    args = [x, g.reshape(1, d), w]
    out_shape = [jax.ShapeDtypeStruct((t, d), F32)] * 3
    out_specs = [pl.BlockSpec((tm, d), row)] * 3
    if gate:
        in_specs += [pl.BlockSpec((d, LANES), const), pl.BlockSpec((1, LANES), const)]
        args += [wf, bf]
        out_shape.append(jax.ShapeDtypeStruct((t, N_HEADS), F32))
        out_specs.append(pl.BlockSpec((tm, N_HEADS), row))
    return pl.pallas_call(
        functools.partial(_norm_proj_rows_kernel, d=d, gate=gate),
        grid=(t // tm,), in_specs=in_specs, out_specs=out_specs, out_shape=out_shape,
        compiler_params=_params("arbitrary"),
        name="norm_proj_rows_gate" if gate else "norm_proj_rows")(*args)


def _aug_constants(tm):
    width = N_HEADS * HEAD_PAD
    place_q = np.zeros((3, LANES, width), np.float32)
    place_k = np.zeros((3, N_HEADS * 16, N_HEADS), np.float32)
    q_ones = np.zeros((1, width), np.float32)
    k_ones = np.zeros((N_HEADS * 16, LANES), np.float32)
    for h in range(N_HEADS):
        for r in range(3):
            place_q[r, h, h * HEAD_PAD + AUG + r] = 1.0
            place_k[r, h * 16 + 3 + r, h] = -1.0
            q_ones[0, h * HEAD_PAD + AUG + 3 + r] = 1.0
            k_ones[h * 16 + r, :] = 1.0
    tri = np.tril(np.ones((tm, tm), np.float32))
    return (jnp.asarray(place_q, BF16), jnp.asarray(place_k, BF16), jnp.asarray(q_ones),
            jnp.asarray(k_ones), jnp.asarray(tri, BF16), jnp.asarray(tri.T, BF16))


def _norm_proj_t_kernel(*refs, d, gate, tm, tiles_per_seq):
    x_ref, g_ref, wq_ref, wkt_ref, wvt_ref = refs[:5]
    refs = refs[5:]
    if gate:
        (wf_ref, bf_ref, wft_ref, bft_ref, pq_ref, pk_ref, qones_ref, kones_ref,
         tri_ref, trit_ref) = refs[:10]
        refs = refs[10:]
    q_ref, kt_ref, vt_ref, ka_ref, va_ref = refs[:5]
    refs = refs[5:]

    hb = _rmsnorm(x_ref[...], g_ref[...]).astype(BF16)
    q = _dot(hb, wq_ref[...]) * (HEAD_DIM ** -0.5)
    kt = _dot_nt(wkt_ref[...], hb)
    vt = _dot_nt(wvt_ref[...], hb)
    kt_ref[...] = kt
    vt_ref[...] = vt
    ktb = kt.astype(BF16)
    vtb = vt.astype(BF16)
    zeros = jnp.zeros((HEAD_PAD - HEAD_DIM - 16, tm), BF16)
    ones_row = (lax.broadcasted_iota(jnp.int32, (16, tm), 0) == 0).astype(BF16)

    if gate:
        logft_ref, carry_ref, carryt_ref = refs[:3]

        @pl.when(pl.program_id(0) % tiles_per_seq == 0)
        def _():
            carry_ref[...] = jnp.zeros_like(carry_ref)
            carryt_ref[...] = jnp.zeros_like(carryt_ref)

        logf = _log_sigmoid(_dot(hb, wf_ref[...]) + bf_ref[...])
        cum = _dot3(tri_ref[...], _split3(logf)) + carry_ref[0:1, :]
        carry_ref[...] = jnp.broadcast_to(cum[-1:, :], carry_ref.shape)
        c1, c2, c3 = _split3(cum)
        q = q + ((_dot(c1, pq_ref[0]) + _dot(c2, pq_ref[1])) + _dot(c3, pq_ref[2])) + qones_ref[...]
        logft = _log_sigmoid(_dot_nt(wft_ref[...], hb) + bft_ref[:, 0:1])
        logft_ref[...] = logft
        cumt = _dot3_right(_split3(logft), trit_ref[...]) + carryt_ref[:, 0:1]
        carryt_ref[...] = jnp.broadcast_to(cumt[:, -1:], carryt_ref.shape)
        t1, t2, t3 = _split3(cumt)
        kaug = ((_dot(pk_ref[0], t1) + _dot(pk_ref[1], t2)) + _dot(pk_ref[2], t3)
                + _lane_tile(kones_ref[...], tm // LANES)).astype(BF16)

    q_ref[...] = q.astype(BF16)
    for h in range(N_HEADS):
        lo, base = h * HEAD_DIM, h * HEAD_PAD
        ka_ref[base:base + HEAD_DIM, :] = ktb[lo:lo + HEAD_DIM, :]
        va_ref[base:base + HEAD_DIM, :] = vtb[lo:lo + HEAD_DIM, :]
        ka_ref[base + HEAD_DIM:base + HEAD_DIM + 16, :] = (
            kaug[h * 16:(h + 1) * 16, :] if gate else jnp.zeros((16, tm), BF16))
        va_ref[base + HEAD_DIM:base + HEAD_DIM + 16, :] = ones_row
        ka_ref[base + HEAD_DIM + 16:base + HEAD_PAD, :] = zeros
        va_ref[base + HEAD_DIM + 16:base + HEAD_PAD, :] = zeros


def _norm_proj_t(x, g, wq_pad, wkt, wvt, gate_args=None, *, tm, seq_len):
    t, d = x.shape
    nb = t // seq_len
    tiles_per_seq = seq_len // tm
    width = N_HEADS * HEAD_PAD
    gate = gate_args is not None
    row = lambda i: (i, 0)
    const = lambda i: (0, 0)
    col = lambda i: (i // tiles_per_seq, 0, i % tiles_per_seq)
    in_specs = [pl.BlockSpec((tm, d), row), pl.BlockSpec((1, d), const),
                pl.BlockSpec((d, width), const), pl.BlockSpec((d, d), const),
                pl.BlockSpec((d, d), const)]
    args = [x, g.reshape(1, d), wq_pad, wkt, wvt]
    out_shape = [jax.ShapeDtypeStruct((t, width), BF16),
                 jax.ShapeDtypeStruct((nb, d, seq_len), F32),
                 jax.ShapeDtypeStruct((nb, d, seq_len), F32),
                 jax.ShapeDtypeStruct((nb, width, seq_len), BF16),
                 jax.ShapeDtypeStruct((nb, width, seq_len), BF16)]
    out_specs = [pl.BlockSpec((tm, width), row), pl.BlockSpec((None, d, tm), col),
                 pl.BlockSpec((None, d, tm), col), pl.BlockSpec((None, width, tm), col),
                 pl.BlockSpec((None, width, tm), col)]
    scratch = []
    if gate:
        wf, bf, wft, bft = gate_args
        consts = _aug_constants(tm)
        args += [wf, bf, wft, bft, *consts]
        in_specs += [pl.BlockSpec((d, LANES), const), pl.BlockSpec((1, LANES), const),
                     pl.BlockSpec((N_HEADS, d), const), pl.BlockSpec((N_HEADS, LANES), const),
                     pl.BlockSpec(consts[0].shape, lambda i: (0, 0, 0)),
                     pl.BlockSpec(consts[1].shape, lambda i: (0, 0, 0)),
                     pl.BlockSpec(consts[2].shape, const), pl.BlockSpec(consts[3].shape, const),
                     pl.BlockSpec((tm, tm), const), pl.BlockSpec((tm, tm), const)]
        out_shape.append(jax.ShapeDtypeStruct((nb, N_HEADS, seq_len), F32))
        out_specs.append(pl.BlockSpec((None, N_HEADS, tm), col))
        scratch = [pltpu.VMEM((8, LANES), F32), pltpu.VMEM((N_HEADS, LANES), F32)]
    return pl.pallas_call(
        functools.partial(_norm_proj_t_kernel, d=d, gate=gate, tm=tm, tiles_per_seq=tiles_per_seq),
        grid=(t // tm,), in_specs=in_specs, out_specs=out_specs, out_shape=out_shape,
        scratch_shapes=scratch, compiler_params=_params("arbitrary"),
        name="norm_proj_t_gate" if gate else "norm_proj_t")(*args)


def _sb_tiles(qs, kts, vts, u, ones, carries, accs, mask, token_major=False, chained=False):
    n = len(kts)
    keys = u.shape[0]
    s = [_dot_nt(qs[i], kts[i]) if token_major else _dot(qs[i], kts[i]) for i in range(n)]
    lk = [-_softplus(x) for x in s]
    masks = mask if isinstance(mask, (list, tuple)) else [mask] * n
    if mask is not None:
        lk = [jnp.where(m, x, 0.0) for m, x in zip(masks, lk)]
    hi = [x.astype(BF16) for x in lk]
    lo = [(x - h.astype(F32)).astype(BF16) for x, h in zip(lk, hi)]
    suffix = [_dot(h, u) + _dot(l, u) for h, l in zip(hi, lo)]
    total = [_dot(h, ones) + _dot(l, ones) for h, l in zip(hi, lo)]
    if chained:
        before = [carries[0]]
        for i in range(n):
            before.append(before[-1] + total[i])
        new_carries = [before[-1]]
    else:
        before = list(carries)
        new_carries = [c + t for c, t in zip(carries, total)]
    a = [jnp.exp(s[i] + suffix[i] + _lane_tile(before[i], keys // LANES)) for i in range(n)]
    if mask is not None:
        a = [jnp.where(m, x, 0.0) for m, x in zip(masks, a)]
    pv = [_dot(a[i].astype(BF16), vts[i]) if token_major else _dot_nt(a[i].astype(BF16), vts[i])
          for i in range(n)]
    if chained:
        acc = accs[0]
        for x in pv:
            acc = acc + x
        return new_carries, [acc]
    return new_carries, [acc + x for acc, x in zip(accs, pv)]


def _sb_prompt_kernel(q_ref, k_ref, v_ref, u_ref, ones_ref, u2_ref, ones2_ref, o_ref, *, tq, group, sub):
    i = pl.program_id(2)
    win, back = u_ref.shape[0], u2_ref.shape[0]
    head = lambda g: slice(g * HEAD_PAD, (g + 1) * HEAD_PAD)
    blocks = range(tq // sub)
    chains = [(g, a) for g in range(group) for a in blocks]
    first_row = [i * tq + a * sub for a in blocks]
    window0 = [jnp.maximum(r0 + sub - win, 0) for r0 in first_row]

    def tile(starts, width, u_r, ones_r, carries, accs, masks):
        starts = [pl.multiple_of(x, LANES) for x in starts]
        carries, accs = _sb_tiles(
            [q_ref[a * sub:(a + 1) * sub, head(g)] for g, a in chains],
            [k_ref[head(g), pl.ds(starts[a], width)] for g, a in chains],
            [v_ref[head(g), pl.ds(starts[a], width)] for g, a in chains],
            u_r[...], ones_r[...], carries, accs, [masks[a] for _, a in chains])
        return tuple(carries), tuple(accs)

    def live(carries):
        top = carries[0]
        for carry in carries[1:]:
            top = jnp.maximum(top, carry)
        return (jnp.max(top) > -SB_EXIT).astype(jnp.int32)

    zero = (jnp.zeros((sub, LANES), F32),) * len(chains)
    row = lax.broadcasted_iota(jnp.int32, (sub, win), 0)
    col = lax.broadcasted_iota(jnp.int32, (sub, win), 1)
    causal = [col < row + (first_row[a] - window0[a]) for a in blocks]
    carries, accs = tile(window0, win, u_ref, ones_ref, zero, zero, causal)
    col2 = lax.broadcasted_iota(jnp.int32, (sub, back), 1)

    def body(st):
        j, _, carries, accs = st
        prev = [jnp.maximum(w - (j - 1) * back, 0) for w in window0]
        start = [jnp.maximum(w - j * back, 0) for w in window0]
        fresh = [col2 < prev[a] - start[a] for a in blocks]
        carries, accs = tile(start, back, u2_ref, ones2_ref, carries, accs, fresh)
        return j + 1, live(carries), carries, accs

    def cond(st):
        j, alive = st[0], st[1]
        uncovered = jnp.maximum(window0[-1] - (j - 1) * back, 0) > 0
        return jnp.logical_and(alive > 0, uncovered)

    accs = lax.while_loop(cond, body, (jnp.int32(1), live(carries), carries, accs))[3]
    for n, (g, a) in enumerate(chains):
        o_ref[a * sub:(a + 1) * sub, head(g)] = accs[n].astype(BF16)


def _suffix_ones(n):
    return jnp.asarray(np.tril(np.ones((n, n), np.float32)), BF16)


def _attn_specs(s, tq, group):
    tile = pl.BlockSpec((None, tq, group * HEAD_PAD), lambda bi, h, i: (bi, i, h))
    resident = pl.BlockSpec((None, group * HEAD_PAD, s), lambda bi, h, i: (bi, h, 0))
    return tile, resident


def _sb_prompt(q, kat, vat, *, tq, group, sub, win, back):
    b, s, width = q.shape
    tile, resident = _attn_specs(s, tq, group)
    const = lambda bi, h, i: (0, 0)
    return pl.pallas_call(
        functools.partial(_sb_prompt_kernel, tq=tq, group=group, sub=sub),
        grid=(b, N_HEADS // group, s // tq),
        in_specs=[tile, resident, resident, pl.BlockSpec((win, win), const),
                  pl.BlockSpec((win, LANES), const), pl.BlockSpec((back, back), const),
                  pl.BlockSpec((back, LANES), const)],
        out_specs=tile, out_shape=jax.ShapeDtypeStruct((b, s, width), BF16),
        compiler_params=_params("arbitrary", "arbitrary", "arbitrary"),
        name="sb_prompt")(q, kat, vat, _suffix_ones(win), jnp.ones((win, LANES), BF16),
                          _suffix_ones(back), jnp.ones((back, LANES), BF16))


def _fox_prompt_kernel(q_ref, k_ref, v_ref, o_ref, m_ref, acc_ref, *, tq, group):
    i = pl.program_id(2)
    row = lax.broadcasted_iota(jnp.int32, (tq, tq), 0)
    col = lax.broadcasted_iota(jnp.int32, (tq, tq), 1)
    m_ref[...] = jnp.full(m_ref.shape, NEG_BIG, F32)
    acc_ref[...] = jnp.zeros_like(acc_ref)
    head = lambda g: slice(g * HEAD_PAD, (g + 1) * HEAD_PAD)

    def tile(j, mask):
        start = pl.multiple_of(j * tq, tq)
        heads = range(group)
        s = [_dot(q_ref[:, head(g)], k_ref[head(g), pl.ds(start, tq)]) for g in heads]
        if mask is not None:
            s = [jnp.where(mask, x, NEG_BIG) for x in s]
        m_prev = [m_ref[g] for g in heads]
        m_next = [jnp.maximum(m_prev[g], jnp.max(s[g], axis=1, keepdims=True)) for g in heads]
        p = [jnp.exp(s[g] - _lane_tile(m_next[g], tq // LANES)).astype(BF16) for g in heads]
        pv = [_dot_nt(p[g], v_ref[head(g), pl.ds(start, tq)]) for g in heads]
        for g in heads:
            acc_ref[g] = acc_ref[g] * jnp.exp(m_prev[g] - m_next[g]) + pv[g]
            m_ref[g] = m_next[g]

    tile(i, col <= row)

    @pl.loop(0, i)
    def _(j):
        tile(j, None)

    for g in range(group):
        acc = acc_ref[g]
        lane = lax.broadcasted_iota(jnp.int32, acc.shape, 1)
        denom = jnp.sum(jnp.where(lane == AUG, acc, 0.0), axis=1, keepdims=True)
        o_ref[:, head(g)] = (acc / denom).astype(BF16)


def _fox_prompt(q, kat, vat, *, tq, group):
    b, s, width = q.shape
    tile, resident = _attn_specs(s, tq, group)
    return pl.pallas_call(
        functools.partial(_fox_prompt_kernel, tq=tq, group=group),
        grid=(b, N_HEADS // group, s // tq),
        in_specs=[tile, resident, resident],
        out_specs=tile, out_shape=jax.ShapeDtypeStruct((b, s, width), BF16),
        scratch_shapes=[pltpu.VMEM((group, tq, LANES), F32), pltpu.VMEM((group, tq, HEAD_PAD), F32)],
        compiler_params=_params("arbitrary", "arbitrary", "arbitrary"),
        name="fox_prompt")(q, kat, vat)


def _head_lane_mask(d):
    r = lax.broadcasted_iota(jnp.int32, (N_HEADS, d), 0)
    lane = lax.broadcasted_iota(jnp.int32, (N_HEADS, d), 1)
    return lane // HEAD_DIM == r


def _block_diag_query(q):
    tokens, d = q.shape
    mask = _head_lane_mask(d)
    rows = [jnp.where(mask, jnp.broadcast_to(q[t:t + 1, :], (N_HEADS, d)), 0.0)
            for t in range(tokens)]
    return jnp.concatenate(rows, axis=0).astype(BF16)


def _collect_heads(acc, tokens):
    mask = _head_lane_mask(acc.shape[1])
    rows = [jnp.sum(jnp.where(mask, acc[t * N_HEADS:(t + 1) * N_HEADS, :], 0.0), axis=0, keepdims=True)
            for t in range(tokens)]
    return jnp.concatenate(rows, axis=0)


def _pad_rows(x, rows):
    return jnp.concatenate([x, jnp.zeros((rows - x.shape[0], x.shape[1]), x.dtype)], axis=0)


def _new_token_mask(rows, page, strict):
    r = lax.broadcasted_iota(jnp.int32, (rows, page), 0) // N_HEADS
    c = lax.broadcasted_iota(jnp.int32, (rows, page), 1)
    return c < r if strict else c <= r


def _sb_decode_kernel(*refs, tokens, page, npp, first):
    q_ref = refs[2 - first]
    refs = refs[3 - first:]
    if first:
        kn_ref, vn_ref = refs[:2]
    else:
        acc_in_ref, carry_in_ref = refs[:2]
    refs = refs[2:]
    kc_refs, vc_refs = refs[:npp], refs[npp:2 * npp]
    u_ref, ones_ref = refs[2 * npp:2 * npp + 2]
    refs = refs[2 * npp + 2:]
    if first:
        acc_ref, carry_ref, qbd_ref, live_ref = refs
    else:
        o_ref, acc_ref, carry_ref, qbd_ref, live_ref = refs
    step = pl.program_id(1)
    rows = tokens * N_HEADS
    u = u_ref[...]
    ones = ones_ref[...]

    def is_live(carry):
        return (jnp.max(carry) > -SB_EXIT).astype(jnp.int32)

    @pl.when(step == 0)
    def _():
        qbd = _block_diag_query(q_ref[...])
        qbd_ref[...] = qbd
        if first:
            (carry,), (acc,) = _sb_tiles(
                [qbd], [_pad_rows(kn_ref[...], page).astype(BF16)],
                [_pad_rows(vn_ref[...], page).astype(BF16)], u, ones,
                [jnp.zeros((rows, LANES), F32)], [jnp.zeros(acc_ref.shape, F32)],
                _new_token_mask(rows, page, True), token_major=True)
        else:
            carry, acc = carry_in_ref[...], acc_in_ref[...]
        acc_ref[...] = acc
        carry_ref[...] = carry
        live_ref[0] = is_live(carry)

    @pl.when(live_ref[0] > 0)
    def _():
        (carry,), (acc,) = _sb_tiles(
            [qbd_ref[...]] * npp, [kc[...].astype(BF16) for kc in kc_refs],
            [vc[...].astype(BF16) for vc in vc_refs], u, ones,
            [carry_ref[...]], [acc_ref[...]], None, chained=True)
        acc_ref[...] = acc
        carry_ref[...] = carry
        live_ref[0] = is_live(carry)

    if not first:
        @pl.when(step == pl.num_programs(1) - 1)
        def _():
            o_ref[...] = _collect_heads(acc_ref[...], tokens)


def _collect_kernel(acc_ref, o_ref, *, tokens):
    o_ref[...] = _collect_heads(acc_ref[...], tokens)


def _sb_decode(q, kn, vn, cache_kt, cache_vt, page_table, *, head_pages, npp):
    b, tokens, d = q.shape
    n_pages = page_table.shape[1]
    page = cache_kt.shape[2]
    rows = tokens * N_HEADS
    u, ones = _suffix_ones(page), jnp.ones((page, LANES), BF16)
    new = pl.BlockSpec((None, tokens, d), lambda bi, s, *_: (bi, 0, 0))
    consts = [pl.BlockSpec((page, page), lambda bi, s, *_: (0, 0)),
              pl.BlockSpec((page, LANES), lambda bi, s, *_: (0, 0))]
    acc_spec = pl.BlockSpec((None, rows, d), lambda bi, s, *_: (bi, 0, 0))
    carry_spec = pl.BlockSpec((None, rows, LANES), lambda bi, s, *_: (bi, 0, 0))
    scratch = [pltpu.VMEM((rows, d), BF16), pltpu.SMEM((1,), jnp.int32)]

    def paged(first_page, gated):
        def make(i):
            def index(bi, s, pt, *live):
                p = first_page - (s * npp + i)
                if gated:
                    p = jnp.where(live[0][bi] > 0, p, first_page)
                return (pt[bi, p], 0, 0)
            return pl.BlockSpec((None, d, page), index)
        return [make(i) for i in range(npp)] * 2

    acc, carry = pl.pallas_call(
        functools.partial(_sb_decode_kernel, tokens=tokens, page=page, npp=npp, first=True),
        grid_spec=pltpu.PrefetchScalarGridSpec(
            num_scalar_prefetch=1, grid=(b, head_pages // npp),
            in_specs=[new, new, new] + paged(n_pages - 1, False) + consts,
            out_specs=[acc_spec, carry_spec], scratch_shapes=scratch),
        out_shape=[jax.ShapeDtypeStruct((b, rows, d), F32), jax.ShapeDtypeStruct((b, rows, LANES), F32)],
        compiler_params=_params("arbitrary", "arbitrary"),
        name="sb_decode_head")(page_table, q, kn, vn, *([cache_kt] * npp), *([cache_vt] * npp), u, ones)
    live = (jnp.max(carry, axis=(1, 2)) > -SB_EXIT).astype(jnp.int32)
    out_shape = jax.ShapeDtypeStruct((b, tokens, d), F32)

    def older_pages():
        return pl.pallas_call(
            functools.partial(_sb_decode_kernel, tokens=tokens, page=page, npp=npp, first=False),
            grid_spec=pltpu.PrefetchScalarGridSpec(
                num_scalar_prefetch=2, grid=(b, (n_pages - head_pages) // npp),
                in_specs=[new, acc_spec, carry_spec] + paged(n_pages - 1 - head_pages, True) + consts,
                out_specs=new,
                scratch_shapes=[pltpu.VMEM((rows, d), F32), pltpu.VMEM((rows, LANES), F32)] + scratch),
            out_shape=out_shape, compiler_params=_params("arbitrary", "arbitrary"),
            name="sb_decode_tail")(page_table, live, q, acc, carry, *([cache_kt] * npp),
                                   *([cache_vt] * npp), u, ones)

    def finished():
        return pl.pallas_call(
            functools.partial(_collect_kernel, tokens=tokens), grid=(b,), in_specs=[pl.BlockSpec((None, rows, d), lambda bi: (bi, 0, 0))],
            out_specs=pl.BlockSpec((None, tokens, d), lambda bi: (bi, 0, 0)),
            out_shape=out_shape, compiler_params=_params("arbitrary"),
            name="sb_decode_done")(acc)

    return lax.cond(jnp.any(live > 0), older_pages, finished)


def _fox_decode_kernel(*refs, tokens, page, npp):
    q_ref, kn_ref, vn_ref, lfn_ref = refs[1:5]
    refs = refs[5:]
    kc_refs, vc_refs, lc_refs = refs[:npp], refs[npp:2 * npp], refs[2 * npp:3 * npp]
    us_ref, ones_ref, o_ref, qbd_ref, acc_ref, m_ref, l_ref, fcarry_ref = refs[3 * npp:]
    step = pl.program_id(1)
    rows = tokens * N_HEADS
    us = us_ref[...]
    ones = ones_ref[...]

    def biases(lgs):
        pieces = [_split3(lg) for lg in lgs]
        suffix = [_dot3_right(p, us) for p in pieces]
        total = [_dot3_right(p, ones) for p in pieces]
        carry = fcarry_ref[...]
        out = []
        for sfx, tot in zip(suffix, total):
            out.append(jnp.concatenate([sfx + carry] * tokens, axis=0))
            carry = carry + tot
        fcarry_ref[...] = carry
        return out

    def update(s_list, pv):
        m_prev = m_ref[...]
        m_next = m_prev
        for s in s_list:
            m_next = jnp.maximum(m_next, jnp.max(s, axis=1, keepdims=True))
        alpha = jnp.exp(m_prev - m_next)
        p = [jnp.exp(s - m_next).astype(BF16) for s in s_list]
        outs = [v(x) for v, x in zip(pv, p)]
        sums = [_dot(x, ones) for x in p]
        acc = acc_ref[...] * _lane_tile(alpha, acc_ref.shape[1] // LANES)
        l = l_ref[...] * alpha
        for o, r in zip(outs, sums):
            acc = acc + o
            l = l + r
        acc_ref[...] = acc
        l_ref[...] = l
        m_ref[...] = m_next

    @pl.when(step == 0)
    def _():
        qbd = _block_diag_query(q_ref[...])
        qbd_ref[...] = qbd
        m_ref[...] = jnp.full(m_ref.shape, NEG_BIG, F32)
        l_ref[...] = jnp.zeros_like(l_ref)
        acc_ref[...] = jnp.zeros_like(acc_ref)
        fcarry_ref[...] = jnp.zeros_like(fcarry_ref)
        lgn = _pad_rows(jnp.concatenate(
            [lfn_ref[...], jnp.zeros((tokens, LANES - N_HEADS), F32)], axis=1), page).T[:N_HEADS, :]
        kn = _pad_rows(kn_ref[...], page).astype(BF16)
        vn = _pad_rows(vn_ref[...], page).astype(BF16)
        s = _dot_nt(qbd, kn) + biases([lgn])[0]
        s = jnp.where(_new_token_mask(rows, page, False), s, NEG_BIG)
        update([s], [lambda p: _dot(p, vn)])

    qbd = qbd_ref[...]
    qk = [_dot(qbd, kc[...].astype(BF16)) for kc in kc_refs]
    bias = biases([lc[...] for lc in lc_refs])
    update([a + b for a, b in zip(qk, bias)],
           [lambda p, vc=vc: _dot_nt(p, vc[...].astype(BF16)) for vc in vc_refs])

    @pl.when(step == pl.num_programs(1) - 1)
    def _():
        out = acc_ref[...] / _lane_tile(l_ref[...], acc_ref.shape[1] // LANES)
        o_ref[...] = _collect_heads(out, tokens)


def _fox_decode(q, kn, vn, lfn, cache_kt, cache_vt, cache_lt, page_table, *, npp):
    b, tokens, d = q.shape
    n_pages = page_table.shape[1]
    page = cache_kt.shape[2]
    rows = tokens * N_HEADS
    new = pl.BlockSpec((None, tokens, d), lambda bi, s, pt: (bi, 0, 0))

    def paged(width):
        def make(i):
            return pl.BlockSpec((None, width, page),
                                lambda bi, s, pt: (pt[bi, n_pages - 1 - (s * npp + i)], 0, 0))
        return [make(i) for i in range(npp)]

    strict = jnp.asarray(np.triu(np.ones((page, page), np.float32), k=1).T, BF16)
    return pl.pallas_call(
        functools.partial(_fox_decode_kernel, tokens=tokens, page=page, npp=npp),
        grid_spec=pltpu.PrefetchScalarGridSpec(
            num_scalar_prefetch=1, grid=(b, n_pages // npp),
            in_specs=[new, new, new, pl.BlockSpec((None, tokens, N_HEADS), lambda bi, s, pt: (bi, 0, 0))]
            + paged(d) + paged(d) + paged(N_HEADS)
            + [pl.BlockSpec((page, page), lambda bi, s, pt: (0, 0)),
               pl.BlockSpec((page, LANES), lambda bi, s, pt: (0, 0))],
            out_specs=new,
            scratch_shapes=[pltpu.VMEM((rows, d), BF16), pltpu.VMEM((rows, d), F32),
                            pltpu.VMEM((rows, LANES), F32), pltpu.VMEM((rows, LANES), F32),
                            pltpu.VMEM((N_HEADS, LANES), F32)]),
        out_shape=jax.ShapeDtypeStruct((b, tokens, d), F32),
        compiler_params=_params("arbitrary", "arbitrary"),
        name="fox_decode")(page_table, q, kn, vn, lfn, *([cache_kt] * npp), *([cache_vt] * npp),
                           *([cache_lt] * npp), strict, jnp.ones((page, LANES), BF16))


def _top2_gates(logits, n_experts):
    lane = lax.broadcasted_iota(jnp.int32, logits.shape, 1).astype(F32)
    lg = jnp.where(lane < n_experts, logits, NEG_BIG)
    m1 = jnp.max(lg, axis=1, keepdims=True)
    i1 = jnp.min(jnp.where(lg == m1, lane, float(LANES)), axis=1, keepdims=True)
    lg2 = jnp.where(lane == i1, NEG_BIG, lg)
    m2 = jnp.max(lg2, axis=1, keepdims=True)
    i2 = jnp.min(jnp.where(lg2 == m2, lane, float(LANES)), axis=1, keepdims=True)
    e2 = jnp.exp(m2 - m1)
    w1 = 1.0 / (1.0 + e2)
    gate = jnp.where(lane == i1, w1, 0.0) + jnp.where(lane == i2, e2 * w1, 0.0)
    return gate, jnp.logical_or(lane == i1, lane == i2)


def _router_logits(h, router):
    logits = None
    r1, r2, r3 = _split3(router)
    h1, h2, h3 = _split3(h)
    for a, b in ((h3, r1), (h2, r2), (h1, r3), (h2, r1), (h1, r2), (h1, r1)):
        t = _dot(a, b)
        logits = t if logits is None else logits + t
    return logits


def _swiglu_partial(hb, wg_ref, wu_ref, wd_ref):
    g = _dot(hb, wg_ref[...])
    u = _dot(hb, wu_ref[...])
    return _dot(((g * jax.nn.sigmoid(g)) * u).astype(BF16), wd_ref[...])


def _ffn_kernel(x_ref, o_ref, wo_ref, g_ref, wg_ref, wu_ref, wd_ref, out_ref, x1_ref, hb_ref, acc_ref):
    f = pl.program_id(1)

    @pl.when(f == 0)
    def _():
        x1 = x_ref[...] + _dot(o_ref[...].astype(BF16), wo_ref[...])
        x1_ref[...] = x1
        hb_ref[...] = _rmsnorm(x1, g_ref[...]).astype(BF16)
        acc_ref[...] = jnp.zeros_like(acc_ref)

    acc_ref[...] += _swiglu_partial(hb_ref[...], wg_ref, wu_ref, wd_ref)

    @pl.when(f == pl.num_programs(1) - 1)
    def _():
        out_ref[...] = x1_ref[...] + acc_ref[...]


def _ffn(x, o, wo, g, wg, wu, wd, *, tm, tf):
    t, d = x.shape
    dff = wg.shape[1]
    row = lambda i, f: (i, 0)
    const = lambda i, f: (0, 0)
    return pl.pallas_call(
        _ffn_kernel, grid=(t // tm, dff // tf),
        in_specs=[pl.BlockSpec((tm, d), row), pl.BlockSpec((tm, o.shape[1]), row),
                  pl.BlockSpec(wo.shape, const), pl.BlockSpec((1, d), const),
                  pl.BlockSpec((d, tf), lambda i, f: (0, f)), pl.BlockSpec((d, tf), lambda i, f: (0, f)),
                  pl.BlockSpec((tf, d), lambda i, f: (f, 0))],
        out_specs=pl.BlockSpec((tm, d), row), out_shape=jax.ShapeDtypeStruct((t, d), F32),
        scratch_shapes=[pltpu.VMEM((tm, d), F32), pltpu.VMEM((tm, d), BF16), pltpu.VMEM((tm, d), F32)],
        compiler_params=_params("arbitrary", "arbitrary"),
        name="dense_ffn")(x, o, wo, g.reshape(1, d), wg, wu, wd)


SEG_ALIGN = 16
MOE_CHUNK = 256
MOE_ROWS = 512


def _moe_pre_kernel(x_ref, o_ref, wo_ref, g_ref, router_ref, lower_ref, upper_ref,
                    x1_ref, hb_ref, gate_ref, rank_ref, rankt_ref, count_ref, *, n_experts):
    x1 = x_ref[...] + _dot(o_ref[...].astype(BF16), wo_ref[...])
    x1_ref[...] = x1
    h = _rmsnorm(x1, g_ref[...])
    hb_ref[...] = h.astype(BF16)
    gate, routed = _top2_gates(_router_logits(h, router_ref[...]), n_experts)
    gate_ref[...] = gate
    mask = routed.astype(F32)
    rank = _dot(lower_ref[...], mask.astype(BF16))
    rank_ref[...] = jnp.where(routed, rank, -1.0)
    mask_t = mask.T
    rank_t = _dot(mask_t.astype(BF16), upper_ref[...])
    rankt_ref[...] = jnp.where(mask_t > 0.0, rank_t, -1.0)[:8, :]
    count_ref[...] = jnp.broadcast_to(jnp.sum(mask, axis=0, keepdims=True), count_ref.shape)


def _moe_dispatch_kernel(seg_ref, hb_ref, rankt_ref, init_ref, xs_ref):
    del seg_ref, init_ref
    e = pl.program_id(1)
    n = hb_ref.shape[0]
    slot = lax.broadcasted_iota(jnp.int32, (n, n), 0).astype(F32)
    pick = (slot == rankt_ref[pl.ds(e, 1), :]).astype(BF16)
    xs_ref[...] = _dot(pick, hb_ref[...]).astype(BF16)


def _moe_expert_kernel(expert_ref, valid_ref, xs_ref, wg_ref, wu_ref, wd_ref, ys_ref, acc_ref):
    del expert_ref
    r, f = pl.program_id(0), pl.program_id(1)

    @pl.when(valid_ref[r] > 0)
    def _():
        @pl.when(f == 0)
        def _():
            acc_ref[...] = jnp.zeros_like(acc_ref)

        acc_ref[...] += _swiglu_partial(xs_ref[...], wg_ref, wu_ref, wd_ref)

        @pl.when(f == pl.num_programs(1) - 1)
        def _():
            ys_ref[...] = acc_ref[...]

    @pl.when(valid_ref[r] == 0)
    def _():
        ys_ref[...] = jnp.zeros_like(ys_ref)


def _moe_combine_kernel(seg_ref, *refs, n_experts):
    del seg_ref
    ys_refs = refs[:n_experts]
    rank_ref, gate_ref, x1_ref, gf_ref, out_ref = refs[n_experts:]
    n = rank_ref.shape[0]
    rank, gate = rank_ref[...], gate_ref[...]
    lane = lax.broadcasted_iota(jnp.int32, rank.shape, 1)
    slot = lax.broadcasted_iota(jnp.int32, (n, n), 1).astype(F32)
    y = jnp.zeros(x1_ref.shape, F32)
    for e, ys_ref in enumerate(ys_refs):
        column = lambda a: jnp.sum(jnp.where(lane == e, a, 0.0), axis=1, keepdims=True)
        pick = (slot == column(rank)).astype(BF16)
        rows = ys_ref[...]
        hi = rows.astype(BF16)
        lo = (rows - hi.astype(F32)).astype(BF16)
        y = y + column(gate) * (_dot(pick, hi) + _dot(pick, lo))
    out_ref[...] = _rmsnorm(x1_ref[...] + y, gf_ref[...])


def _moe(x, o, wo, g, router, wg, wu, wd, g_final, *, tf):
    t, d = x.shape
    n_experts, _, dff = wg.shape
    chunk = _tile(t, MOE_CHUNK)
    n_chunks = t // chunk
    row = lambda i: (i, 0)
    const = lambda i: (0, 0)
    lower = jnp.asarray(np.tril(np.ones((chunk, chunk), np.float32), k=-1), BF16)
    x1, hb, gate, rank, rank_t, counts = pl.pallas_call(
        functools.partial(_moe_pre_kernel, n_experts=n_experts),
        grid=(n_chunks,),
        in_specs=[pl.BlockSpec((chunk, d), row), pl.BlockSpec((chunk, o.shape[1]), row),
                  pl.BlockSpec(wo.shape, const), pl.BlockSpec((1, d), const),
                  pl.BlockSpec((d, LANES), const), pl.BlockSpec((chunk, chunk), const),
                  pl.BlockSpec((chunk, chunk), const)],
        out_specs=[pl.BlockSpec((chunk, d), row), pl.BlockSpec((chunk, d), row),
                   pl.BlockSpec((chunk, LANES), row), pl.BlockSpec((chunk, LANES), row),
                   pl.BlockSpec((None, 8, chunk), lambda i: (i, 0, 0)),
                   pl.BlockSpec((None, 8, LANES), lambda i: (i, 0, 0))],
        out_shape=[jax.ShapeDtypeStruct((t, d), F32), jax.ShapeDtypeStruct((t, d), BF16),
                   jax.ShapeDtypeStruct((t, LANES), F32), jax.ShapeDtypeStruct((t, LANES), F32),
                   jax.ShapeDtypeStruct((n_chunks, 8, chunk), F32),
                   jax.ShapeDtypeStruct((n_chunks, 8, LANES), F32)],
        compiler_params=_params("arbitrary"),
        name="moe_pre")(x, o, wo, g.reshape(1, d), router, lower, lower.T)

    rows_tile = MOE_ROWS
    n = counts[:, 0, :n_experts].astype(jnp.int32)
    n_pad = (n + SEG_ALIGN - 1) // SEG_ALIGN * SEG_ALIGN
    cap = (jnp.sum(n_pad, axis=0) + chunk + rows_tile - 1) // rows_tile * rows_tile
    region_end = jnp.cumsum(cap)
    region_start = region_end - cap
    seg = (region_start[None, :] + jnp.cumsum(n_pad, axis=0) - n_pad).reshape(-1)
    seg = seg // SEG_ALIGN
    total_rows = (2 * t + n_chunks * n_experts * (SEG_ALIGN - 1) + n_experts * (chunk + rows_tile - 1))
    n_tiles = -(-total_rows // rows_tile)
    tile_start = jnp.arange(n_tiles, dtype=jnp.int32) * rows_tile
    tile_valid = (tile_start < region_end[-1]).astype(jnp.int32)
    tile_expert = jnp.minimum(jnp.searchsorted(region_end, tile_start, side="right"),
                              n_experts - 1).astype(jnp.int32)

    seg_index = lambda c, e, seg_ref: (seg_ref[c * n_experts + e] * SEG_ALIGN, 0)
    xs = pl.pallas_call(
        _moe_dispatch_kernel,
        grid_spec=pltpu.PrefetchScalarGridSpec(
            num_scalar_prefetch=1, grid=(n_chunks, n_experts),
            in_specs=[pl.BlockSpec((chunk, d), lambda c, e, s: (c, 0)),
                      pl.BlockSpec((None, 8, chunk), lambda c, e, s: (c, 0, 0)),
                      pl.BlockSpec(memory_space=pl.ANY)],
            out_specs=pl.BlockSpec((pl.Element(chunk), pl.Element(d)), seg_index)),
        out_shape=jax.ShapeDtypeStruct((n_tiles * rows_tile, d), BF16),
        input_output_aliases={3: 0},
        compiler_params=_params("arbitrary", "arbitrary"),
        name="moe_dispatch")(seg, hb, rank_t, jnp.zeros((n_tiles * rows_tile, d), BF16))

    n_f = dff // tf
    f_index = lambda r, f, valid: jnp.where(valid[r] > 0, f, n_f - 1)
    ys = pl.pallas_call(
        _moe_expert_kernel,
        grid_spec=pltpu.PrefetchScalarGridSpec(
            num_scalar_prefetch=2, grid=(n_tiles, n_f),
            in_specs=[pl.BlockSpec((rows_tile, d), lambda r, f, ex, va: (r, 0)),
                      pl.BlockSpec((None, d, tf), lambda r, f, ex, va: (ex[r], 0, f_index(r, f, va))),
                      pl.BlockSpec((None, d, tf), lambda r, f, ex, va: (ex[r], 0, f_index(r, f, va))),
                      pl.BlockSpec((None, tf, d), lambda r, f, ex, va: (ex[r], f_index(r, f, va), 0))],
            out_specs=pl.BlockSpec((rows_tile, d), lambda r, f, ex, va: (r, 0)),
            scratch_shapes=[pltpu.VMEM((rows_tile, d), F32)]),
        out_shape=jax.ShapeDtypeStruct((n_tiles * rows_tile, d), F32),
        compiler_params=_params("arbitrary", "arbitrary"),
        name="moe_experts")(tile_expert, tile_valid, xs, wg, wu, wd)

    def ys_spec(e):
        return pl.BlockSpec((pl.Element(chunk), pl.Element(d)), lambda c, s: (s[c * n_experts + e] * SEG_ALIGN, 0))

    return pl.pallas_call(
        functools.partial(_moe_combine_kernel, n_experts=n_experts),
        grid_spec=pltpu.PrefetchScalarGridSpec(
            num_scalar_prefetch=1, grid=(n_chunks,),
            in_specs=[ys_spec(e) for e in range(n_experts)]
            + [pl.BlockSpec((chunk, LANES), lambda c, s: (c, 0)),
               pl.BlockSpec((chunk, LANES), lambda c, s: (c, 0)),
               pl.BlockSpec((chunk, d), lambda c, s: (c, 0)),
               pl.BlockSpec((1, d), lambda c, s: (0, 0))],
            out_specs=pl.BlockSpec((chunk, d), lambda c, s: (c, 0))),
        out_shape=jax.ShapeDtypeStruct((t, d), F32),
        compiler_params=_params("arbitrary"),
        name="moe_combine")(seg, *([ys] * n_experts), rank, gate, x1, g_final.reshape(1, d))


def _tile(n, want):
    while n % want:
        want //= 2
    return want


def _lane_tile_size(n, want):
    return next(c for c in range(want, 0, -LANES) if n % c == 0)


def _pad_heads(w, axis):
    shape = w.shape[:axis] + (N_HEADS, HEAD_DIM) + w.shape[axis + 1:]
    pad = [(0, 0)] * (len(shape))
    pad[axis + 1] = (0, HEAD_PAD - HEAD_DIM)
    padded = jnp.pad(w.reshape(shape), pad)
    return padded.reshape(w.shape[:axis] + (N_HEADS * HEAD_PAD,) + w.shape[axis + 1:])


def kernel(x_prompt, x_sample, cache_sb_k, cache_sb_v, cache_fox_k, cache_fox_v, cache_fox_logf,
           page_table, ln_attn0, w_in0, w_o0, ln_ffn0, w_gate0, w_up0, w_down0, ln_attn1, w_in1,
           b_f1, w_o1, ln_ffn1, router1, w_gate1, w_up1, w_down1, ln_final):
    b, s, d = x_prompt.shape
    db, ds, _ = x_sample.shape
    n_pool, page = cache_sb_k.shape[:2]
    n_pages = page_table.shape[1]
    bf = lambda w: w.astype(BF16)

    def attn_weights(w_in, w_o):
        w = bf(w_in[:, :3 * d])
        return (w, _pad_heads(w[:, :d], 1), w[:, d:2 * d].T, w[:, 2 * d:].T,
                bf(w_o), _pad_heads(bf(w_o), 0))

    w0_rows, wq0, wkt0, wvt0, wo0, wo0_pad = attn_weights(w_in0, w_o0)
    w1_rows, wq1, wkt1, wvt1, wo1, wo1_pad = attn_weights(w_in1, w_o1)
    wf_cols = bf(w_in1[:, 3 * d:])
    wf = jnp.pad(wf_cols, ((0, 0), (0, LANES - N_HEADS)))
    bf_row = jnp.pad(b_f1, (0, LANES - N_HEADS)).reshape(1, LANES)
    gate_args = (wf, bf_row, wf_cols.T, jnp.broadcast_to(b_f1[:, None], (N_HEADS, LANES)))
    ffn0 = (bf(w_gate0), bf(w_up0), bf(w_down0))
    ffn1 = (bf(w_gate1), bf(w_up1), bf(w_down1))
    router = jnp.pad(router1, ((0, 0), (0, LANES - router1.shape[1])))
    paged_t = lambda c: jnp.transpose(c, (0, 2, 3, 1)).reshape(n_pool, d, page)
    logf_t = jnp.transpose(cache_fox_logf, (0, 2, 1))

    tp, ts = b * s, db * ds
    tm_p, tm_s = _tile(s, 256), _tile(ts, 256)
    tq_sb, tq_fox = _tile(s, 256), _tile(s, 512)
    sub_sb = LANES
    win_sb = min(2 * LANES, s)
    assert tq_sb % sub_sb == 0 and win_sb % LANES == 0 and win_sb >= sub_sb
    tf = _lane_tile_size(w_gate0.shape[1], 896)
    tmf_p, tmf_s = _tile(tp, 512), _tile(ts, 256)
    pages_per_step = lambda most: next(c for c in (16, 8, 4, 2, 1)
                                       if c <= most and n_pages % c == 0 and n_pages >= 2 * c)
    npp, npp_fox = pages_per_step(8), pages_per_step(16)
    xp, xs = x_prompt.reshape(tp, d), x_sample.reshape(ts, d)
    seq = lambda a: a.reshape(db, ds, -1)
    width = N_HEADS * HEAD_PAD
    heads_out = lambda t: jnp.transpose(t.reshape(b, N_HEADS, HEAD_DIM, s), (0, 3, 1, 2))

    qp, k0t, v0t, kat, vat = _norm_proj_t(xp, ln_attn0, wq0, wkt0, wvt0, tm=tm_p, seq_len=s)
    op = _sb_prompt(qp.reshape(b, s, width), kat, vat, tq=tq_sb, group=4, sub=sub_sb, win=win_sb, back=LANES)
    xp = _ffn(xp, op.reshape(tp, width), wo0_pad, ln_ffn0, *ffn0, tm=tmf_p, tf=tf)

    qs, k0s, v0s = _norm_proj_rows(xs, ln_attn0, w0_rows, tm=tm_s)
    os_ = _sb_decode(seq(qs), seq(k0s), seq(v0s), paged_t(cache_sb_k), paged_t(cache_sb_v),
                     page_table, head_pages=npp, npp=npp)
    xs = _ffn(xs, os_.reshape(ts, d), wo0, ln_ffn0, *ffn0, tm=tmf_s, tf=tf)

    qp, k1t, v1t, kat, vat, lft = _norm_proj_t(xp, ln_attn1, wq1, wkt1, wvt1, gate_args,
                                               tm=tm_p, seq_len=s)
    op = _fox_prompt(qp.reshape(b, s, width), kat, vat, tq=tq_fox, group=4)
    yp = _moe(xp, op.reshape(tp, width), wo1_pad, ln_ffn1, router, *ffn1, ln_final, tf=tf)

    qs, k1s, v1s, lfs = _norm_proj_rows(xs, ln_attn1, w1_rows, wf, bf_row, tm=tm_s)
    os_ = _fox_decode(seq(qs), seq(k1s), seq(v1s), seq(lfs), paged_t(cache_fox_k),
                      paged_t(cache_fox_v), logf_t, page_table, npp=npp_fox)
    ys = _moe(xs, os_.reshape(ts, d), wo1, ln_ffn1, router, *ffn1, ln_final, tf=tf)

    hs = lambda a: a.reshape(db, ds, N_HEADS, HEAD_DIM)
    return (yp.reshape(b, s, d), ys.reshape(db, ds, d), heads_out(k0t), heads_out(v0t),
            heads_out(k1t), heads_out(v1t), jnp.transpose(lft, (0, 2, 1)),
            hs(k0s), hs(v0s), hs(k1s), hs(v1s), seq(lfs))
```

```python
import functools

import numpy as np
import jax
import jax.numpy as jnp
from jax import lax
from jax.experimental import pallas as pl
from jax.experimental.pallas import tpu as pltpu

N_HEADS = 16
HEAD_DIM = 64
LANES = 128
HEAD_PAD = 128
AUG = HEAD_DIM
RMS_EPS = 1e-6
NEG_BIG = -1e30
SB_EXIT = 90.0
VMEM_LIMIT = 56 * 1024 * 1024

F32 = jnp.float32
BF16 = jnp.bfloat16

_NT = (((1,), (1,)), ((), ()))


def _dot(a, b):
    return jnp.dot(a, b, preferred_element_type=F32)


def _dot_nt(a, b):
    return lax.dot_general(a, b, _NT, preferred_element_type=F32)


def _split3(x):
    p1 = x.astype(BF16)
    r1 = x - p1.astype(F32)
    p2 = r1.astype(BF16)
    r2 = r1 - p2.astype(F32)
    return p1, p2, r2.astype(BF16)


def _dot3(a, pieces):
    return (_dot(a, pieces[0]) + _dot(a, pieces[1])) + _dot(a, pieces[2])


def _dot3_right(pieces, b):
    return (_dot(pieces[0], b) + _dot(pieces[1], b)) + _dot(pieces[2], b)


def _softplus(z):
    return jnp.maximum(z, 0.0) + jnp.log1p(jnp.exp(-jnp.abs(z)))


def _log_sigmoid(z):
    return jnp.minimum(z, 0.0) - jnp.log1p(jnp.exp(-jnp.abs(z)))


def _rmsnorm(x, g):
    ms = jnp.mean(x * x, axis=-1, keepdims=True)
    return (x * lax.rsqrt(ms + RMS_EPS)) * g


def _lane_tile(x, n):
    return x if n == 1 else jnp.concatenate([x] * n, axis=1)


def _params(*sem):
    return pltpu.CompilerParams(dimension_semantics=sem, vmem_limit_bytes=VMEM_LIMIT)


def _norm_proj_rows_kernel(*refs, d, gate):
    x_ref, g_ref, w_ref = refs[:3]
    refs = refs[3:]
    if gate:
        wf_ref, bf_ref = refs[:2]
        refs = refs[2:]
    q_ref, k_ref, v_ref = refs[:3]
    hb = _rmsnorm(x_ref[...], g_ref[...]).astype(BF16)
    y = _dot(hb, w_ref[...])
    q_ref[...] = y[:, :d] * (HEAD_DIM ** -0.5)
    k_ref[...] = y[:, d:2 * d]
    v_ref[...] = y[:, 2 * d:]
    if gate:
        lg = _dot(hb, wf_ref[...]) + bf_ref[...]
        refs[3][...] = _log_sigmoid(lg)[:, :N_HEADS]


def _norm_proj_rows(x, g, w, wf=None, bf=None, *, tm):
    t, d = x.shape
    gate = wf is not None
    row = lambda i: (i, 0)
    const = lambda i: (0, 0)
    in_specs = [pl.BlockSpec((tm, d), row), pl.BlockSpec((1, d), const),
                pl.BlockSpec((d, 3 * d), const)]
    args = [x, g.reshape(1, d), w]
    out_shape = [jax.ShapeDtypeStruct((t, d), F32)] * 3
    out_specs = [pl.BlockSpec((tm, d), row)] * 3
    if gate:
        in_specs += [pl.BlockSpec((d, LANES), const), pl.BlockSpec((1, LANES), const)]
        args += [wf, bf]
        out_shape.append(jax.ShapeDtypeStruct((t, N_HEADS), F32))
        out_specs.append(pl.BlockSpec((tm, N_HEADS), row))
    return pl.pallas_call(
        functools.partial(_norm_proj_rows_kernel, d=d, gate=gate),
        grid=(t // tm,), in_specs=in_specs, out_specs=out_specs, out_shape=out_shape,
        compiler_params=_params("arbitrary"),
        name="norm_proj_rows_gate" if gate else "norm_proj_rows")(*args)


def _aug_constants(tm):
    width = N_HEADS * HEAD_PAD
    place_q = np.zeros((LANES, width), np.float32)
    place_k = np.zeros((3, N_HEADS * 16, N_HEADS), np.float32)
    q_ones = np.zeros((1, width), np.float32)
    k_ones = np.zeros((N_HEADS * 16, LANES), np.float32)
    for h in range(N_HEADS):
        for r in range(3):
            place_q[r * N_HEADS + h, h * HEAD_PAD + AUG + r] = 1.0
            place_k[r, h * 16 + 3 + r, h] = -1.0
            q_ones[0, h * HEAD_PAD + AUG + 3 + r] = 1.0
            k_ones[h * 16 + r, :] = 1.0
    tri = np.tril(np.ones((tm, tm), np.float32))
    return (jnp.asarray(place_q, BF16), jnp.asarray(place_k, BF16), jnp.asarray(q_ones),
            jnp.asarray(k_ones), jnp.asarray(tri, BF16), jnp.asarray(tri.T, BF16))


def _norm_proj_t_kernel(*refs, d, gate, tm, tiles_per_seq):
    x_ref, g_ref, wq_ref, wkt_ref, wvt_ref = refs[:5]
    refs = refs[5:]
    if gate:
        (wf_ref, bf_ref, wft_ref, bft_ref, pq_ref, pk_ref, qones_ref, kones_ref,
         tri_ref, trit_ref) = refs[:10]
        refs = refs[10:]
    q_ref, kt_ref, vt_ref, ka_ref, va_ref = refs[:5]
    refs = refs[5:]

    hb = _rmsnorm(x_ref[...], g_ref[...]).astype(BF16)
    scale = HEAD_DIM ** -0.5
    zeros = jnp.zeros((HEAD_PAD - HEAD_DIM - 16, tm), BF16)
    ones_row = (lax.broadcasted_iota(jnp.int32, (16, tm), 0) == 0).astype(BF16)

    if gate:
        logft_ref, carry_ref, carryt_ref = refs[:3]

        @pl.when(pl.program_id(0) % tiles_per_seq == 0)
        def _():
            carry_ref[...] = jnp.zeros_like(carry_ref)
            carryt_ref[...] = jnp.zeros_like(carryt_ref)

        lg = _dot(hb, wf_ref[...])
        lgt = _dot_nt(wft_ref[...], hb)
        q = _dot(hb, wq_ref[...]) * scale
        logf = _log_sigmoid(lg + bf_ref[...])
        logft = _log_sigmoid(lgt + bft_ref[:, 0:1])
        logft_ref[...] = logft
        cum = _dot3(tri_ref[...], _split3(logf)) + carry_ref[0:1, :]
        cumt = _dot3_right(_split3(logft), trit_ref[...]) + carryt_ref[:, 0:1]
        kt = _dot_nt(wkt_ref[...], hb)
        carry_ref[...] = jnp.broadcast_to(cum[-1:, :], carry_ref.shape)
        carryt_ref[...] = jnp.broadcast_to(cumt[:, -1:], carryt_ref.shape)
        c1, c2, c3 = _split3(cum)
        t1, t2, t3 = _split3(cumt)
        heads_only = lax.broadcasted_iota(jnp.int32, cum.shape, 1) < N_HEADS
        piece = lambda c: jnp.where(heads_only, c.astype(F32), 0.0)
        packed = (piece(c1) + pltpu.roll(piece(c2), N_HEADS, 1)
                  + pltpu.roll(piece(c3), 2 * N_HEADS, 1)).astype(BF16)
        q = q + _dot(packed, pq_ref[...]) + qones_ref[...]
        kaug = ((_dot(pk_ref[0], t1) + _dot(pk_ref[1], t2)) + _dot(pk_ref[2], t3)
                + _lane_tile(kones_ref[...], tm // LANES)).astype(BF16)
        vt = _dot_nt(wvt_ref[...], hb)
    else:
        q = _dot(hb, wq_ref[...]) * scale
        kt = _dot_nt(wkt_ref[...], hb)
        vt = _dot_nt(wvt_ref[...], hb)
    kt_ref[...] = kt
    vt_ref[...] = vt
    ktb = kt.astype(BF16)
    vtb = vt.astype(BF16)
    q_ref[...] = q.astype(BF16)
    for h in range(N_HEADS):
        lo, base = h * HEAD_DIM, h * HEAD_PAD
        ka_ref[base:base + HEAD_DIM, :] = ktb[lo:lo + HEAD_DIM, :]
        va_ref[base:base + HEAD_DIM, :] = vtb[lo:lo + HEAD_DIM, :]
        ka_ref[base + HEAD_DIM:base + HEAD_DIM + 16, :] = (
            kaug[h * 16:(h + 1) * 16, :] if gate else jnp.zeros((16, tm), BF16))
        va_ref[base + HEAD_DIM:base + HEAD_DIM + 16, :] = ones_row
        ka_ref[base + HEAD_DIM + 16:base + HEAD_PAD, :] = zeros
        va_ref[base + HEAD_DIM + 16:base + HEAD_PAD, :] = zeros


def _norm_proj_t(x, g, wq_pad, wkt, wvt, gate_args=None, *, tm, seq_len):
    t, d = x.shape
    nb = t // seq_len
    tiles_per_seq = seq_len // tm
    width = N_HEADS * HEAD_PAD
    gate = gate_args is not None
    row = lambda i: (i, 0)
    const = lambda i: (0, 0)
    col = lambda i: (i // tiles_per_seq, 0, i % tiles_per_seq)
    in_specs = [pl.BlockSpec((tm, d), row), pl.BlockSpec((1, d), const),
                pl.BlockSpec((d, width), const), pl.BlockSpec((d, d), const),
                pl.BlockSpec((d, d), const)]
    args = [x, g.reshape(1, d), wq_pad, wkt, wvt]
    out_shape = [jax.ShapeDtypeStruct((t, width), BF16),
                 jax.ShapeDtypeStruct((nb, d, seq_len), F32),
                 jax.ShapeDtypeStruct((nb, d, seq_len), F32),
                 jax.ShapeDtypeStruct((nb, width, seq_len), BF16),
                 jax.ShapeDtypeStruct((nb, width, seq_len), BF16)]
    out_specs = [pl.BlockSpec((tm, width), row), pl.BlockSpec((None, d, tm), col),
                 pl.BlockSpec((None, d, tm), col), pl.BlockSpec((None, width, tm), col),
                 pl.BlockSpec((None, width, tm), col)]
    scratch = []
    if gate:
        wf, bf, wft, bft = gate_args
        consts = _aug_constants(tm)
        args += [wf, bf, wft, bft, *consts]
        in_specs += [pl.BlockSpec((d, LANES), const), pl.BlockSpec((1, LANES), const),
                     pl.BlockSpec((N_HEADS, d), const), pl.BlockSpec((N_HEADS, LANES), const),
                     pl.BlockSpec(consts[0].shape, const),
                     pl.BlockSpec(consts[1].shape, lambda i: (0, 0, 0)),
                     pl.BlockSpec(consts[2].shape, const), pl.BlockSpec(consts[3].shape, const),
                     pl.BlockSpec((tm, tm), const), pl.BlockSpec((tm, tm), const)]
        out_shape.append(jax.ShapeDtypeStruct((nb, N_HEADS, seq_len), F32))
        out_specs.append(pl.BlockSpec((None, N_HEADS, tm), col))
        scratch = [pltpu.VMEM((8, LANES), F32), pltpu.VMEM((N_HEADS, LANES), F32)]
    return pl.pallas_call(
        functools.partial(_norm_proj_t_kernel, d=d, gate=gate, tm=tm, tiles_per_seq=tiles_per_seq),
        grid=(t // tm,), in_specs=in_specs, out_specs=out_specs, out_shape=out_shape,
        scratch_shapes=scratch, compiler_params=_params("arbitrary"),
        name="norm_proj_t_gate" if gate else "norm_proj_t")(*args)


def _sb_tiles(qs, kts, vts, u, ones, carries, accs, mask, token_major=False, chained=False):
    n = len(kts)
    keys = u.shape[0]
    s = [_dot_nt(qs[i], kts[i]) if token_major else _dot(qs[i], kts[i]) for i in range(n)]
    lk = [-_softplus(x) for x in s]
    masks = mask if isinstance(mask, (list, tuple)) else [mask] * n
    if mask is not None:
        lk = [jnp.where(m, x, 0.0) for m, x in zip(masks, lk)]
    hi = [x.astype(BF16) for x in lk]
    lo = [(x - h.astype(F32)).astype(BF16) for x, h in zip(lk, hi)]
    suffix = [_dot(h, u) + _dot(l, u) for h, l in zip(hi, lo)]
    total = [_dot(h, ones) + _dot(l, ones) for h, l in zip(hi, lo)]
    if chained:
        before = [carries[0]]
        for i in range(n):
            before.append(before[-1] + total[i])
        new_carries = [before[-1]]
    else:
        before = list(carries)
        new_carries = [c + t for c, t in zip(carries, total)]
    a = [jnp.exp(s[i] + suffix[i] + _lane_tile(before[i], keys // LANES)) for i in range(n)]
    if mask is not None:
        a = [jnp.where(m, x, 0.0) for m, x in zip(masks, a)]
    pv = [_dot(a[i].astype(BF16), vts[i]) if token_major else _dot_nt(a[i].astype(BF16), vts[i])
          for i in range(n)]
    if chained:
        acc = accs[0]
        for x in pv:
            acc = acc + x
        return new_carries, [acc]
    return new_carries, [acc + x for acc, x in zip(accs, pv)]


def _sb_prompt_kernel(q_ref, k_ref, v_ref, u_ref, ones_ref, u2_ref, ones2_ref, o_ref, *, tq, group, sub):
    i = pl.program_id(2)
    win, back = u_ref.shape[0], u2_ref.shape[0]
    head = lambda g: slice(g * HEAD_PAD, (g + 1) * HEAD_PAD)
    blocks = range(tq // sub)
    chains = [(g, a) for g in range(group) for a in blocks]
    first_row = [i * tq + a * sub for a in blocks]
    window0 = [jnp.maximum(r0 + sub - win, 0) for r0 in first_row]

    def tile(starts, width, u_r, ones_r, carries, accs, masks):
        starts = [pl.multiple_of(x, LANES) for x in starts]
        carries, accs = _sb_tiles(
            [q_ref[a * sub:(a + 1) * sub, head(g)] for g, a in chains],
            [k_ref[head(g), pl.ds(starts[a], width)] for g, a in chains],
            [v_ref[head(g), pl.ds(starts[a], width)] for g, a in chains],
            u_r[...], ones_r[...], carries, accs, [masks[a] for _, a in chains])
        return tuple(carries), tuple(accs)

    def live(carries):
        top = carries[0]
        for carry in carries[1:]:
            top = jnp.maximum(top, carry)
        return (jnp.max(top) > -SB_EXIT).astype(jnp.int32)

    zero = (jnp.zeros((sub, LANES), F32),) * len(chains)
    row = lax.broadcasted_iota(jnp.int32, (sub, win), 0)
    col = lax.broadcasted_iota(jnp.int32, (sub, win), 1)
    causal = [col < row + (first_row[a] - window0[a]) for a in blocks]
    carries, accs = tile(window0, win, u_ref, ones_ref, zero, zero, causal)
    col2 = lax.broadcasted_iota(jnp.int32, (sub, back), 1)

    def body(st):
        j, _, carries, accs = st
        prev = [jnp.maximum(w - (j - 1) * back, 0) for w in window0]
        start = [jnp.maximum(w - j * back, 0) for w in window0]
        fresh = [col2 < prev[a] - start[a] for a in blocks]
        carries, accs = tile(start, back, u2_ref, ones2_ref, carries, accs, fresh)
        return j + 1, live(carries), carries, accs

    def cond(st):
        j, alive = st[0], st[1]
        uncovered = jnp.maximum(window0[-1] - (j - 1) * back, 0) > 0
        return jnp.logical_and(alive > 0, uncovered)

    accs = lax.while_loop(cond, body, (jnp.int32(1), live(carries), carries, accs))[3]
    for n, (g, a) in enumerate(chains):
        o_ref[a * sub:(a + 1) * sub, head(g)] = accs[n].astype(BF16)


def _suffix_ones(n):
    return jnp.asarray(np.tril(np.ones((n, n), np.float32)), BF16)


def _attn_specs(s, tq, group):
    tile = pl.BlockSpec((None, tq, group * HEAD_PAD), lambda bi, h, i: (bi, i, h))
    resident = pl.BlockSpec((None, group * HEAD_PAD, s), lambda bi, h, i: (bi, h, 0))
    return tile, resident


def _sb_prompt(q, kat, vat, *, tq, group, sub, win, back):
    b, s, width = q.shape
    tile, resident = _attn_specs(s, tq, group)
    const = lambda bi, h, i: (0, 0)
    return pl.pallas_call(
        functools.partial(_sb_prompt_kernel, tq=tq, group=group, sub=sub),
        grid=(b, N_HEADS // group, s // tq),
        in_specs=[tile, resident, resident, pl.BlockSpec((win, win), const),
                  pl.BlockSpec((win, LANES), const), pl.BlockSpec((back, back), const),
                  pl.BlockSpec((back, LANES), const)],
        out_specs=tile, out_shape=jax.ShapeDtypeStruct((b, s, width), BF16),
        compiler_params=_params("arbitrary", "arbitrary", "arbitrary"),
        name="sb_prompt")(q, kat, vat, _suffix_ones(win), jnp.ones((win, LANES), BF16),
                          _suffix_ones(back), jnp.ones((back, LANES), BF16))


def _fox_prompt_kernel(q_ref, k_ref, v_ref, o_ref, m_ref, acc_ref, *, tq, group):
    i = pl.program_id(2)
    row = lax.broadcasted_iota(jnp.int32, (tq, tq), 0)
    col = lax.broadcasted_iota(jnp.int32, (tq, tq), 1)
    m_ref[...] = jnp.full(m_ref.shape, NEG_BIG, F32)
    acc_ref[...] = jnp.zeros_like(acc_ref)
    head = lambda g: slice(g * HEAD_PAD, (g + 1) * HEAD_PAD)

    def tile(j, mask):
        start = pl.multiple_of(j * tq, tq)
        heads = range(group)
        s = [_dot(q_ref[:, head(g)], k_ref[head(g), pl.ds(start, tq)]) for g in heads]
        if mask is not None:
            s = [jnp.where(mask, x, NEG_BIG) for x in s]
        m_prev = [m_ref[g] for g in heads]
        m_next = [jnp.maximum(m_prev[g], jnp.max(s[g], axis=1, keepdims=True)) for g in heads]
        p = [jnp.exp(s[g] - _lane_tile(m_next[g], tq // LANES)).astype(BF16) for g in heads]
        pv = [_dot_nt(p[g], v_ref[head(g), pl.ds(start, tq)]) for g in heads]
        for g in heads:
            acc_ref[g] = acc_ref[g] * jnp.exp(m_prev[g] - m_next[g]) + pv[g]
            m_ref[g] = m_next[g]

    tile(i, col <= row)

    @pl.loop(0, i)
    def _(j):
        tile(j, None)

    for g in range(group):
        acc = acc_ref[g]
        lane = lax.broadcasted_iota(jnp.int32, acc.shape, 1)
        denom = jnp.sum(jnp.where(lane == AUG, acc, 0.0), axis=1, keepdims=True)
        o_ref[:, head(g)] = (acc / denom).astype(BF16)


def _fox_prompt(q, kat, vat, *, tq, group):
    b, s, width = q.shape
    tile, resident = _attn_specs(s, tq, group)
    return pl.pallas_call(
        functools.partial(_fox_prompt_kernel, tq=tq, group=group),
        grid=(b, N_HEADS // group, s // tq),
        in_specs=[tile, resident, resident],
        out_specs=tile, out_shape=jax.ShapeDtypeStruct((b, s, width), BF16),
        scratch_shapes=[pltpu.VMEM((group, tq, LANES), F32), pltpu.VMEM((group, tq, HEAD_PAD), F32)],
        compiler_params=_params("arbitrary", "arbitrary", "arbitrary"),
        name="fox_prompt")(q, kat, vat)


def _head_lane_mask(d):
    r = lax.broadcasted_iota(jnp.int32, (N_HEADS, d), 0)
    lane = lax.broadcasted_iota(jnp.int32, (N_HEADS, d), 1)
    return lane // HEAD_DIM == r


def _block_diag_query(q):
    tokens, d = q.shape
    mask = _head_lane_mask(d)
    rows = [jnp.where(mask, jnp.broadcast_to(q[t:t + 1, :], (N_HEADS, d)), 0.0)
            for t in range(tokens)]
    return jnp.concatenate(rows, axis=0).astype(BF16)


def _collect_heads(acc, tokens):
    mask = _head_lane_mask(acc.shape[1])
    rows = [jnp.sum(jnp.where(mask, acc[t * N_HEADS:(t + 1) * N_HEADS, :], 0.0), axis=0, keepdims=True)
            for t in range(tokens)]
    return jnp.concatenate(rows, axis=0)


def _pad_rows(x, rows):
    return jnp.concatenate([x, jnp.zeros((rows - x.shape[0], x.shape[1]), x.dtype)], axis=0)


def _new_token_mask(rows, page, strict):
    r = lax.broadcasted_iota(jnp.int32, (rows, page), 0) // N_HEADS
    c = lax.broadcasted_iota(jnp.int32, (rows, page), 1)
    return c < r if strict else c <= r


def _sb_decode_kernel(*refs, tokens, page, npp, first):
    q_ref = refs[2 - first]
    refs = refs[3 - first:]
    if first:
        kn_ref, vn_ref = refs[:2]
    else:
        acc_in_ref, carry_in_ref = refs[:2]
    refs = refs[2:]
    kc_refs, vc_refs = refs[:npp], refs[npp:2 * npp]
    u_ref, ones_ref = refs[2 * npp:2 * npp + 2]
    refs = refs[2 * npp + 2:]
    if first:
        acc_ref, carry_ref, qbd_ref, live_ref = refs
    else:
        o_ref, acc_ref, carry_ref, qbd_ref, live_ref = refs
    step = pl.program_id(1)
    rows = tokens * N_HEADS
    u = u_ref[...]
    ones = ones_ref[...]

    def is_live(carry):
        return (jnp.max(carry) > -SB_EXIT).astype(jnp.int32)

    @pl.when(step == 0)
    def _():
        qbd = _block_diag_query(q_ref[...])
        qbd_ref[...] = qbd
        if first:
            (carry,), (acc,) = _sb_tiles(
                [qbd], [_pad_rows(kn_ref[...], page).astype(BF16)],
                [_pad_rows(vn_ref[...], page).astype(BF16)], u, ones,
                [jnp.zeros((rows, LANES), F32)], [jnp.zeros(acc_ref.shape, F32)],
                _new_token_mask(rows, page, True), token_major=True)
        else:
            carry, acc = carry_in_ref[...], acc_in_ref[...]
        acc_ref[...] = acc
        carry_ref[...] = carry
        live_ref[0] = is_live(carry)

    @pl.when(live_ref[0] > 0)
    def _():
        (carry,), (acc,) = _sb_tiles(
            [qbd_ref[...]] * npp, [kc[...].astype(BF16) for kc in kc_refs],
            [vc[...].astype(BF16) for vc in vc_refs], u, ones,
            [carry_ref[...]], [acc_ref[...]], None, chained=True)
        acc_ref[...] = acc
        carry_ref[...] = carry
        live_ref[0] = is_live(carry)

    if not first:
        @pl.when(step == pl.num_programs(1) - 1)
        def _():
            o_ref[...] = _collect_heads(acc_ref[...], tokens)


def _collect_kernel(acc_ref, o_ref, *, tokens):
    o_ref[...] = _collect_heads(acc_ref[...], tokens)


def _sb_decode(q, kn, vn, cache_kt, cache_vt, page_table, *, head_pages, npp):
    b, tokens, d = q.shape
    n_pages = page_table.shape[1]
    page = cache_kt.shape[2]
    rows = tokens * N_HEADS
    u, ones = _suffix_ones(page), jnp.ones((page, LANES), BF16)
    new = pl.BlockSpec((None, tokens, d), lambda bi, s, *_: (bi, 0, 0))
    consts = [pl.BlockSpec((page, page), lambda bi, s, *_: (0, 0)),
              pl.BlockSpec((page, LANES), lambda bi, s, *_: (0, 0))]
    acc_spec = pl.BlockSpec((None, rows, d), lambda bi, s, *_: (bi, 0, 0))
    carry_spec = pl.BlockSpec((None, rows, LANES), lambda bi, s, *_: (bi, 0, 0))
    scratch = [pltpu.VMEM((rows, d), BF16), pltpu.SMEM((1,), jnp.int32)]

    def paged(first_page, gated):
        def make(i):
            def index(bi, s, pt, *live):
                p = first_page - (s * npp + i)
                if gated:
                    p = jnp.where(live[0][bi] > 0, p, first_page)
                return (pt[bi, p], 0, 0)
            return pl.BlockSpec((None, d, page), index)
        return [make(i) for i in range(npp)] * 2

    acc, carry = pl.pallas_call(
        functools.partial(_sb_decode_kernel, tokens=tokens, page=page, npp=npp, first=True),
        grid_spec=pltpu.PrefetchScalarGridSpec(
            num_scalar_prefetch=1, grid=(b, head_pages // npp),
            in_specs=[new, new, new] + paged(n_pages - 1, False) + consts,
            out_specs=[acc_spec, carry_spec], scratch_shapes=scratch),
        out_shape=[jax.ShapeDtypeStruct((b, rows, d), F32), jax.ShapeDtypeStruct((b, rows, LANES), F32)],
        compiler_params=_params("arbitrary", "arbitrary"),
        name="sb_decode_head")(page_table, q, kn, vn, *([cache_kt] * npp), *([cache_vt] * npp), u, ones)
    live = (jnp.max(carry, axis=(1, 2)) > -SB_EXIT).astype(jnp.int32)
    out_shape = jax.ShapeDtypeStruct((b, tokens, d), F32)

    def older_pages():
        return pl.pallas_call(
            functools.partial(_sb_decode_kernel, tokens=tokens, page=page, npp=npp, first=False),
            grid_spec=pltpu.PrefetchScalarGridSpec(
                num_scalar_prefetch=2, grid=(b, (n_pages - head_pages) // npp),
                in_specs=[new, acc_spec, carry_spec] + paged(n_pages - 1 - head_pages, True) + consts,
                out_specs=new,
                scratch_shapes=[pltpu.VMEM((rows, d), F32), pltpu.VMEM((rows, LANES), F32)] + scratch),
            out_shape=out_shape, compiler_params=_params("arbitrary", "arbitrary"),
            name="sb_decode_tail")(page_table, live, q, acc, carry, *([cache_kt] * npp),
                                   *([cache_vt] * npp), u, ones)

    def finished():
        return pl.pallas_call(
            functools.partial(_collect_kernel, tokens=tokens), grid=(b,), in_specs=[pl.BlockSpec((None, rows, d), lambda bi: (bi, 0, 0))],
            out_specs=pl.BlockSpec((None, tokens, d), lambda bi: (bi, 0, 0)),
            out_shape=out_shape, compiler_params=_params("arbitrary"),
            name="sb_decode_done")(acc)

    return lax.cond(jnp.any(live > 0), older_pages, finished)


def _fox_decode_kernel(*refs, tokens, page, npp):
    q_ref, kn_ref, vn_ref, lfn_ref = refs[1:5]
    refs = refs[5:]
    kc_refs, vc_refs, lc_refs = refs[:npp], refs[npp:2 * npp], refs[2 * npp:3 * npp]
    us_ref, ones_ref, o_ref, qbd_ref, acc_ref, m_ref, l_ref, fcarry_ref = refs[3 * npp:]
    step = pl.program_id(1)
    rows = tokens * N_HEADS
    us = us_ref[...]
    ones = ones_ref[...]

    def biases(lgs):
        pieces = [_split3(lg) for lg in lgs]
        suffix = [_dot3_right(p, us) for p in pieces]
        total = [_dot3_right(p, ones) for p in pieces]
        carry = fcarry_ref[...]
        out = []
        for sfx, tot in zip(suffix, total):
            out.append(jnp.concatenate([sfx + carry] * tokens, axis=0))
            carry = carry + tot
        fcarry_ref[...] = carry
        return out

    def update(s_list, pv):
        m_prev = m_ref[...]
        m_next = m_prev
        for s in s_list:
            m_next = jnp.maximum(m_next, jnp.max(s, axis=1, keepdims=True))
        alpha = jnp.exp(m_prev - m_next)
        p = [jnp.exp(s - m_next).astype(BF16) for s in s_list]
        outs = [v(x) for v, x in zip(pv, p)]
        sums = [_dot(x, ones) for x in p]
        acc = acc_ref[...] * _lane_tile(alpha, acc_ref.shape[1] // LANES)
        l = l_ref[...] * alpha
        for o, r in zip(outs, sums):
            acc = acc + o
            l = l + r
        acc_ref[...] = acc
        l_ref[...] = l
        m_ref[...] = m_next

    @pl.when(step == 0)
    def _():
        qbd = _block_diag_query(q_ref[...])
        qbd_ref[...] = qbd
        m_ref[...] = jnp.full(m_ref.shape, NEG_BIG, F32)
        l_ref[...] = jnp.zeros_like(l_ref)
        acc_ref[...] = jnp.zeros_like(acc_ref)
        fcarry_ref[...] = jnp.zeros_like(fcarry_ref)
        lgn = _pad_rows(jnp.concatenate(
            [lfn_ref[...], jnp.zeros((tokens, LANES - N_HEADS), F32)], axis=1), page).T[:N_HEADS, :]
        kn = _pad_rows(kn_ref[...], page).astype(BF16)
        vn = _pad_rows(vn_ref[...], page).astype(BF16)
        s = _dot_nt(qbd, kn) + biases([lgn])[0]
        s = jnp.where(_new_token_mask(rows, page, False), s, NEG_BIG)
        update([s], [lambda p: _dot(p, vn)])

    qbd = qbd_ref[...]
    qk = [_dot(qbd, kc[...].astype(BF16)) for kc in kc_refs]
    bias = biases([lc[...] for lc in lc_refs])
    update([a + b for a, b in zip(qk, bias)],
           [lambda p, vc=vc: _dot_nt(p, vc[...].astype(BF16)) for vc in vc_refs])

    @pl.when(step == pl.num_programs(1) - 1)
    def _():
        out = acc_ref[...] / _lane_tile(l_ref[...], acc_ref.shape[1] // LANES)
        o_ref[...] = _collect_heads(out, tokens)


def _fox_decode(q, kn, vn, lfn, cache_kt, cache_vt, cache_lt, page_table, *, npp):
    b, tokens, d = q.shape
    n_pages = page_table.shape[1]
    page = cache_kt.shape[2]
    rows = tokens * N_HEADS
    new = pl.BlockSpec((None, tokens, d), lambda bi, s, pt: (bi, 0, 0))

    def paged(width):
        def make(i):
            return pl.BlockSpec((None, width, page),
                                lambda bi, s, pt: (pt[bi, n_pages - 1 - (s * npp + i)], 0, 0))
        return [make(i) for i in range(npp)]

    strict = jnp.asarray(np.triu(np.ones((page, page), np.float32), k=1).T, BF16)
    return pl.pallas_call(
        functools.partial(_fox_decode_kernel, tokens=tokens, page=page, npp=npp),
        grid_spec=pltpu.PrefetchScalarGridSpec(
            num_scalar_prefetch=1, grid=(b, n_pages // npp),
            in_specs=[new, new, new, pl.BlockSpec((None, tokens, N_HEADS), lambda bi, s, pt: (bi, 0, 0))]
            + paged(d) + paged(d) + paged(N_HEADS)
            + [pl.BlockSpec((page, page), lambda bi, s, pt: (0, 0)),
               pl.BlockSpec((page, LANES), lambda bi, s, pt: (0, 0))],
            out_specs=new,
            scratch_shapes=[pltpu.VMEM((rows, d), BF16), pltpu.VMEM((rows, d), F32),
                            pltpu.VMEM((rows, LANES), F32), pltpu.VMEM((rows, LANES), F32),
                            pltpu.VMEM((N_HEADS, LANES), F32)]),
        out_shape=jax.ShapeDtypeStruct((b, tokens, d), F32),
        compiler_params=_params("arbitrary", "arbitrary"),
        name="fox_decode")(page_table, q, kn, vn, lfn, *([cache_kt] * npp), *([cache_vt] * npp),
                           *([cache_lt] * npp), strict, jnp.ones((page, LANES), BF16))


def _top2_gates(logits, n_experts):
    lane = lax.broadcasted_iota(jnp.int32, logits.shape, 1).astype(F32)
    lg = jnp.where(lane < n_experts, logits, NEG_BIG)
    m1 = jnp.max(lg, axis=1, keepdims=True)
    i1 = jnp.min(jnp.where(lg == m1, lane, float(LANES)), axis=1, keepdims=True)
    lg2 = jnp.where(lane == i1, NEG_BIG, lg)
    m2 = jnp.max(lg2, axis=1, keepdims=True)
    i2 = jnp.min(jnp.where(lg2 == m2, lane, float(LANES)), axis=1, keepdims=True)
    e2 = jnp.exp(m2 - m1)
    w1 = 1.0 / (1.0 + e2)
    gate = jnp.where(lane == i1, w1, 0.0) + jnp.where(lane == i2, e2 * w1, 0.0)
    return gate, jnp.logical_or(lane == i1, lane == i2)


def _router_logits(h, router):
    logits = None
    r1, r2, r3 = _split3(router)
    h1, h2, h3 = _split3(h)
    for a, b in ((h3, r1), (h2, r2), (h1, r3), (h2, r1), (h1, r2), (h1, r1)):
        t = _dot(a, b)
        logits = t if logits is None else logits + t
    return logits


def _swiglu_partial(hb, wg_ref, wu_ref, wd_ref):
    g = _dot(hb, wg_ref[...])
    u = _dot(hb, wu_ref[...])
    return _dot(((g * jax.nn.sigmoid(g)) * u).astype(BF16), wd_ref[...])


def _ffn_kernel(x_ref, o_ref, wo_ref, g_ref, wg_ref, wu_ref, wd_ref, out_ref, x1_ref, hb_ref, acc_ref):
    f = pl.program_id(1)

    @pl.when(f == 0)
    def _():
        x1 = x_ref[...] + _dot(o_ref[...].astype(BF16), wo_ref[...])
        x1_ref[...] = x1
        hb_ref[...] = _rmsnorm(x1, g_ref[...]).astype(BF16)
        acc_ref[...] = jnp.zeros_like(acc_ref)

    acc_ref[...] += _swiglu_partial(hb_ref[...], wg_ref, wu_ref, wd_ref)

    @pl.when(f == pl.num_programs(1) - 1)
    def _():
        out_ref[...] = x1_ref[...] + acc_ref[...]


def _ffn(x, o, wo, g, wg, wu, wd, *, tm, tf):
    t, d = x.shape
    dff = wg.shape[1]
    row = lambda i, f: (i, 0)
    const = lambda i, f: (0, 0)
    return pl.pallas_call(
        _ffn_kernel, grid=(t // tm, dff // tf),
        in_specs=[pl.BlockSpec((tm, d), row), pl.BlockSpec((tm, o.shape[1]), row),
                  pl.BlockSpec(wo.shape, const), pl.BlockSpec((1, d), const),
                  pl.BlockSpec((d, tf), lambda i, f: (0, f)), pl.BlockSpec((d, tf), lambda i, f: (0, f)),
                  pl.BlockSpec((tf, d), lambda i, f: (f, 0))],
        out_specs=pl.BlockSpec((tm, d), row), out_shape=jax.ShapeDtypeStruct((t, d), F32),
        scratch_shapes=[pltpu.VMEM((tm, d), F32), pltpu.VMEM((tm, d), BF16), pltpu.VMEM((tm, d), F32)],
        compiler_params=_params("arbitrary", "arbitrary"),
        name="dense_ffn")(x, o, wo, g.reshape(1, d), wg, wu, wd)


SEG_ALIGN = 16
MOE_CHUNK = 256
MOE_ROWS = 512


def _moe_pre_kernel(x_ref, o_ref, wo_ref, g_ref, router_ref, lower_ref, upper_ref,
                    x1_ref, hb_ref, gate_ref, rank_ref, rankt_ref, count_ref, *, n_experts):
    x1 = x_ref[...] + _dot(o_ref[...].astype(BF16), wo_ref[...])
    x1_ref[...] = x1
    h = _rmsnorm(x1, g_ref[...])
    hb_ref[...] = h.astype(BF16)
    gate, routed = _top2_gates(_router_logits(h, router_ref[...]), n_experts)
    gate_ref[...] = gate
    mask = routed.astype(F32)
    rank = _dot(lower_ref[...], mask.astype(BF16))
    rank_ref[...] = jnp.where(routed, rank, -1.0)
    mask_t = mask.T
    rank_t = _dot(mask_t.astype(BF16), upper_ref[...])
    rankt_ref[...] = jnp.where(mask_t > 0.0, rank_t, -1.0)[:8, :]
    count_ref[...] = jnp.broadcast_to(jnp.sum(mask, axis=0, keepdims=True), count_ref.shape)


def _moe_dispatch_kernel(seg_ref, hb_ref, rankt_ref, init_ref, xs_ref):
    del seg_ref, init_ref
    e = pl.program_id(1)
    n = hb_ref.shape[0]
    slot = lax.broadcasted_iota(jnp.int32, (n, n), 0).astype(F32)
    pick = (slot == rankt_ref[pl.ds(e, 1), :]).astype(BF16)
    xs_ref[...] = _dot(pick, hb_ref[...]).astype(BF16)


def _moe_expert_kernel(expert_ref, valid_ref, xs_ref, wg_ref, wu_ref, wd_ref, ys_ref, acc_ref):
    del expert_ref
    r, f = pl.program_id(0), pl.program_id(1)

    @pl.when(valid_ref[r] > 0)
    def _():
        @pl.when(f == 0)
        def _():
            acc_ref[...] = jnp.zeros_like(acc_ref)

        acc_ref[...] += _swiglu_partial(xs_ref[...], wg_ref, wu_ref, wd_ref)

        @pl.when(f == pl.num_programs(1) - 1)
        def _():
            ys_ref[...] = acc_ref[...]

    @pl.when(valid_ref[r] == 0)
    def _():
        ys_ref[...] = jnp.zeros_like(ys_ref)


def _moe_combine_kernel(seg_ref, *refs, n_experts):
    del seg_ref
    ys_refs = refs[:n_experts]
    rank_ref, gate_ref, x1_ref, gf_ref, out_ref = refs[n_experts:]
    n = rank_ref.shape[0]
    rank, gate = rank_ref[...], gate_ref[...]
    lane = lax.broadcasted_iota(jnp.int32, rank.shape, 1)
    slot = lax.broadcasted_iota(jnp.int32, (n, n), 1).astype(F32)
    y = jnp.zeros(x1_ref.shape, F32)
    for e, ys_ref in enumerate(ys_refs):
        column = lambda a: jnp.sum(jnp.where(lane == e, a, 0.0), axis=1, keepdims=True)
        pick = (slot == column(rank)).astype(BF16)
        rows = ys_ref[...]
        hi = rows.astype(BF16)
        lo = (rows - hi.astype(F32)).astype(BF16)
        y = y + column(gate) * (_dot(pick, hi) + _dot(pick, lo))
    out_ref[...] = _rmsnorm(x1_ref[...] + y, gf_ref[...])


def _moe(x, o, wo, g, router, wg, wu, wd, g_final, *, tf):
    t, d = x.shape
    n_experts, _, dff = wg.shape
    chunk = _tile(t, MOE_CHUNK)
    n_chunks = t // chunk
    row = lambda i: (i, 0)
    const = lambda i: (0, 0)
    lower = jnp.asarray(np.tril(np.ones((chunk, chunk), np.float32), k=-1), BF16)
    x1, hb, gate, rank, rank_t, counts = pl.pallas_call(
        functools.partial(_moe_pre_kernel, n_experts=n_experts),
        grid=(n_chunks,),
        in_specs=[pl.BlockSpec((chunk, d), row), pl.BlockSpec((chunk, o.shape[1]), row),
                  pl.BlockSpec(wo.shape, const), pl.BlockSpec((1, d), const),
                  pl.BlockSpec((d, LANES), const), pl.BlockSpec((chunk, chunk), const),
                  pl.BlockSpec((chunk, chunk), const)],
        out_specs=[pl.BlockSpec((chunk, d), row), pl.BlockSpec((chunk, d), row),
                   pl.BlockSpec((chunk, LANES), row), pl.BlockSpec((chunk, LANES), row),
                   pl.BlockSpec((None, 8, chunk), lambda i: (i, 0, 0)),
                   pl.BlockSpec((None, 8, LANES), lambda i: (i, 0, 0))],
        out_shape=[jax.ShapeDtypeStruct((t, d), F32), jax.ShapeDtypeStruct((t, d), BF16),
                   jax.ShapeDtypeStruct((t, LANES), F32), jax.ShapeDtypeStruct((t, LANES), F32),
                   jax.ShapeDtypeStruct((n_chunks, 8, chunk), F32),
                   jax.ShapeDtypeStruct((n_chunks, 8, LANES), F32)],
        compiler_params=_params("arbitrary"),
        name="moe_pre")(x, o, wo, g.reshape(1, d), router, lower, lower.T)

    rows_tile = MOE_ROWS
    n = counts[:, 0, :n_experts].astype(jnp.int32)
    n_pad = (n + SEG_ALIGN - 1) // SEG_ALIGN * SEG_ALIGN
    cap = (jnp.sum(n_pad, axis=0) + chunk + rows_tile - 1) // rows_tile * rows_tile
    region_end = jnp.cumsum(cap)
    region_start = region_end - cap
    seg = (region_start[None, :] + jnp.cumsum(n_pad, axis=0) - n_pad).reshape(-1)
    seg = seg // SEG_ALIGN
    total_rows = (2 * t + n_chunks * n_experts * (SEG_ALIGN - 1) + n_experts * (chunk + rows_tile - 1))
    n_tiles = -(-total_rows // rows_tile)
    tile_start = jnp.arange(n_tiles, dtype=jnp.int32) * rows_tile
    tile_valid = (tile_start < region_end[-1]).astype(jnp.int32)
    tile_expert = jnp.minimum(jnp.searchsorted(region_end, tile_start, side="right"),
                              n_experts - 1).astype(jnp.int32)

    seg_index = lambda c, e, seg_ref: (seg_ref[c * n_experts + e] * SEG_ALIGN, 0)
    xs = pl.pallas_call(
        _moe_dispatch_kernel,
        grid_spec=pltpu.PrefetchScalarGridSpec(
            num_scalar_prefetch=1, grid=(n_chunks, n_experts),
            in_specs=[pl.BlockSpec((chunk, d), lambda c, e, s: (c, 0)),
                      pl.BlockSpec((None, 8, chunk), lambda c, e, s: (c, 0, 0)),
                      pl.BlockSpec(memory_space=pl.ANY)],
            out_specs=pl.BlockSpec((pl.Element(chunk), pl.Element(d)), seg_index)),
        out_shape=jax.ShapeDtypeStruct((n_tiles * rows_tile, d), BF16),
        input_output_aliases={3: 0},
        compiler_params=_params("arbitrary", "arbitrary"),
        name="moe_dispatch")(seg, hb, rank_t, jnp.zeros((n_tiles * rows_tile, d), BF16))

    n_f = dff // tf
    f_index = lambda r, f, valid: jnp.where(valid[r] > 0, f, n_f - 1)
    ys = pl.pallas_call(
        _moe_expert_kernel,
        grid_spec=pltpu.PrefetchScalarGridSpec(
            num_scalar_prefetch=2, grid=(n_tiles, n_f),
            in_specs=[pl.BlockSpec((rows_tile, d), lambda r, f, ex, va: (r, 0)),
                      pl.BlockSpec((None, d, tf), lambda r, f, ex, va: (ex[r], 0, f_index(r, f, va))),
                      pl.BlockSpec((None, d, tf), lambda r, f, ex, va: (ex[r], 0, f_index(r, f, va))),
                      pl.BlockSpec((None, tf, d), lambda r, f, ex, va: (ex[r], f_index(r, f, va), 0))],
            out_specs=pl.BlockSpec((rows_tile, d), lambda r, f, ex, va: (r, 0)),
            scratch_shapes=[pltpu.VMEM((rows_tile, d), F32)]),
        out_shape=jax.ShapeDtypeStruct((n_tiles * rows_tile, d), F32),
        compiler_params=_params("arbitrary", "arbitrary"),
        name="moe_experts")(tile_expert, tile_valid, xs, wg, wu, wd)

    def ys_spec(e):
        return pl.BlockSpec((pl.Element(chunk), pl.Element(d)), lambda c, s: (s[c * n_experts + e] * SEG_ALIGN, 0))

    return pl.pallas_call(
        functools.partial(_moe_combine_kernel, n_experts=n_experts),
        grid_spec=pltpu.PrefetchScalarGridSpec(
            num_scalar_prefetch=1, grid=(n_chunks,),
            in_specs=[ys_spec(e) for e in range(n_experts)]
            + [pl.BlockSpec((chunk, LANES), lambda c, s: (c, 0)),
               pl.BlockSpec((chunk, LANES), lambda c, s: (c, 0)),
               pl.BlockSpec((chunk, d), lambda c, s: (c, 0)),
               pl.BlockSpec((1, d), lambda c, s: (0, 0))],
            out_specs=pl.BlockSpec((chunk, d), lambda c, s: (c, 0))),
        out_shape=jax.ShapeDtypeStruct((t, d), F32),
        compiler_params=_params("arbitrary"),
        name="moe_combine")(seg, *([ys] * n_experts), rank, gate, x1, g_final.reshape(1, d))


def _tile(n, want):
    while n % want:
        want //= 2
    return want


def _lane_tile_size(n, want):
    return next(c for c in range(want, 0, -LANES) if n % c == 0)


def _pad_heads(w, axis):
    shape = w.shape[:axis] + (N_HEADS, HEAD_DIM) + w.shape[axis + 1:]
    pad = [(0, 0)] * (len(shape))
    pad[axis + 1] = (0, HEAD_PAD - HEAD_DIM)
    padded = jnp.pad(w.reshape(shape), pad)
    return padded.reshape(w.shape[:axis] + (N_HEADS * HEAD_PAD,) + w.shape[axis + 1:])


def kernel(x_prompt, x_sample, cache_sb_k, cache_sb_v, cache_fox_k, cache_fox_v, cache_fox_logf,
           page_table, ln_attn0, w_in0, w_o0, ln_ffn0, w_gate0, w_up0, w_down0, ln_attn1, w_in1,
           b_f1, w_o1, ln_ffn1, router1, w_gate1, w_up1, w_down1, ln_final):
    b, s, d = x_prompt.shape
    db, ds, _ = x_sample.shape
    n_pool, page = cache_sb_k.shape[:2]
    n_pages = page_table.shape[1]
    bf = lambda w: w.astype(BF16)

    def attn_weights(w_in, w_o):
        w = bf(w_in[:, :3 * d])
        return (w, _pad_heads(w[:, :d], 1), w[:, d:2 * d].T, w[:, 2 * d:].T,
                bf(w_o), _pad_heads(bf(w_o), 0))

    w0_rows, wq0, wkt0, wvt0, wo0, wo0_pad = attn_weights(w_in0, w_o0)
    w1_rows, wq1, wkt1, wvt1, wo1, wo1_pad = attn_weights(w_in1, w_o1)
    wf_cols = bf(w_in1[:, 3 * d:])
    wf = jnp.pad(wf_cols, ((0, 0), (0, LANES - N_HEADS)))
    bf_row = jnp.pad(b_f1, (0, LANES - N_HEADS)).reshape(1, LANES)
    gate_args = (wf, bf_row, wf_cols.T, jnp.broadcast_to(b_f1[:, None], (N_HEADS, LANES)))
    ffn0 = (bf(w_gate0), bf(w_up0), bf(w_down0))
    ffn1 = (bf(w_gate1), bf(w_up1), bf(w_down1))
    router = jnp.pad(router1, ((0, 0), (0, LANES - router1.shape[1])))
    paged_t = lambda c: jnp.transpose(c, (0, 2, 3, 1)).reshape(n_pool, d, page)
    logf_t = jnp.transpose(cache_fox_logf, (0, 2, 1))

    tp, ts = b * s, db * ds
    tm_p, tm_s = _tile(s, 256), _tile(ts, 256)
    tq_sb, tq_fox = _tile(s, 256), _tile(s, 512)
    sub_sb = LANES
    win_sb = min(2 * LANES, s)
    assert tq_sb % sub_sb == 0 and win_sb % LANES == 0 and win_sb >= sub_sb
    tf = _lane_tile_size(w_gate0.shape[1], 896)
    tmf_p, tmf_s = _tile(tp, 512), _tile(ts, 256)
    pages_per_step = lambda most: next(c for c in (16, 8, 4, 2, 1)
                                       if c <= most and n_pages % c == 0 and n_pages >= 2 * c)
    npp, npp_fox = pages_per_step(8), pages_per_step(16)
    xp, xs = x_prompt.reshape(tp, d), x_sample.reshape(ts, d)
    seq = lambda a: a.reshape(db, ds, -1)
    width = N_HEADS * HEAD_PAD
    heads_out = lambda t: jnp.transpose(t.reshape(b, N_HEADS, HEAD_DIM, s), (0, 3, 1, 2))

    qp, k0t, v0t, kat, vat = _norm_proj_t(xp, ln_attn0, wq0, wkt0, wvt0, tm=tm_p, seq_len=s)
    op = _sb_prompt(qp.reshape(b, s, width), kat, vat, tq=tq_sb, group=4, sub=sub_sb, win=win_sb, back=LANES)
    xp = _ffn(xp, op.reshape(tp, width), wo0_pad, ln_ffn0, *ffn0, tm=tmf_p, tf=tf)

    qs, k0s, v0s = _norm_proj_rows(xs, ln_attn0, w0_rows, tm=tm_s)
    os_ = _sb_decode(seq(qs), seq(k0s), seq(v0s), paged_t(cache_sb_k), paged_t(cache_sb_v),
                     page_table, head_pages=npp, npp=npp)
    xs = _ffn(xs, os_.reshape(ts, d), wo0, ln_ffn0, *ffn0, tm=tmf_s, tf=tf)

    qp, k1t, v1t, kat, vat, lft = _norm_proj_t(xp, ln_attn1, wq1, wkt1, wvt1, gate_args,
                                               tm=tm_p, seq_len=s)
    op = _fox_prompt(qp.reshape(b, s, width), kat, vat, tq=tq_fox, group=4)
    yp = _moe(xp, op.reshape(tp, width), wo1_pad, ln_ffn1, router, *ffn1, ln_final, tf=tf)

    qs, k1s, v1s, lfs = _norm_proj_rows(xs, ln_attn1, w1_rows, wf, bf_row, tm=tm_s)
    os_ = _fox_decode(seq(qs), seq(k1s), seq(v1s), seq(lfs), paged_t(cache_fox_k),
                      paged_t(cache_fox_v), logf_t, page_table, npp=npp_fox)
    ys = _moe(xs, os_.reshape(ts, d), wo1, ln_ffn1, router, *ffn1, ln_final, tf=tf)

    hs = lambda a: a.reshape(db, ds, N_HEADS, HEAD_DIM)
    return (yp.reshape(b, s, d), ys.reshape(db, ds, d), heads_out(k0t), heads_out(v0t),
            heads_out(k1t), heads_out(v1t), jnp.transpose(lft, (0, 2, 1)),
            hs(k0s), hs(v0s), hs(k1s), hs(v1s), seq(lfs))
```
